```python
import jax, jax.numpy as jnp
from jax import lax
import numpy as np

D_MODEL = 1024
BATCH = 2
SEQ = 16384
DEPTH = 4

HEAD_DIM = 64
BLOCK = 128
A_Q_HEADS = 8
A_KV_HEADS = 2
A_WINDOW = 128
B_Q_HEADS = 8
B_KV_HEADS = 2
CMP_LEN = 32
CMP_STRIDE = 16
CMP_HIDDEN = 128
SLC_LEN = 64
N_SLC = 16
N_LOCAL_SLC = 2
B_WINDOW = 512
D_FF = 2816
CONV_WIDTH = 3
ROPE_THETA = 10000.0
EPS = 1e-6
TINY = 1e-30

A_WIDTH = A_Q_HEADS * HEAD_DIM
A_KV_WIDTH = A_KV_HEADS * HEAD_DIM
B_WIDTH = B_Q_HEADS * HEAD_DIM
B_KV_WIDTH = B_KV_HEADS * HEAD_DIM
N_NSA_GATES = 3 * B_Q_HEADS
IN_SIZES = (A_WIDTH, A_KV_WIDTH, A_KV_WIDTH,
            B_WIDTH, B_KV_WIDTH, B_KV_WIDTH, B_KV_WIDTH, B_KV_WIDTH, B_KV_WIDTH, B_KV_WIDTH,
            N_NSA_GATES, D_MODEL, D_MODEL)
IN_COLS = sum(IN_SIZES)

kernel_name = 'hybrid_swa_sink_nsa_convffn'


def rms_norm(x, gain):
    xf = x.astype(jnp.float32)
    y = xf * lax.rsqrt(jnp.mean(xf * xf, axis=-1, keepdims=True) + EPS)
    return (y * gain.astype(jnp.float32)).astype(x.dtype)


def apply_rope(x, positions):
    half = HEAD_DIM // 2
    inv_freq = ROPE_THETA ** (-jnp.arange(half, dtype=jnp.float32) / half)
    ang = positions.astype(jnp.float32)[..., None] * inv_freq
    cos = jnp.cos(ang)[:, :, None, :]
    sin = jnp.sin(ang)[:, :, None, :]
    xf = x.astype(jnp.float32)
    x1, x2 = xf[..., :half], xf[..., half:]
    return jnp.concatenate([x1 * cos - x2 * sin, x2 * cos + x1 * sin], axis=-1).astype(x.dtype)


def masked_softmax(s, mask):
    s = jnp.where(mask, s, -jnp.inf)
    m = jnp.max(s, axis=-1, keepdims=True)
    m = jnp.where(jnp.isfinite(m), m, 0.0)
    p = jnp.exp(s - m)
    return p / jnp.maximum(jnp.sum(p, axis=-1, keepdims=True), TINY)


def band_context(t, n_prev):
    nb = t.shape[1]
    pad = [(0, 0)] * t.ndim
    pad[1] = (n_prev, 0)
    tp = jnp.pad(t, pad)
    return jnp.concatenate([tp[:, j:j + nb] for j in range(n_prev + 1)], axis=2)


def swa_sink_attention(q, k, v, sinks):
    b, s, g, r, d = q.shape
    nb = s // BLOCK
    n_prev = -(-A_WINDOW // BLOCK)
    qb = q.reshape(b, nb, BLOCK, g, r, d)
    kc = band_context(k.reshape(b, nb, BLOCK, g, d), n_prev)
    vc = band_context(v.reshape(b, nb, BLOCK, g, d), n_prev)
    scores = jnp.einsum('bnqgrd,bnkgd->bngrqk', qb, kc).astype(jnp.float32) * (d ** -0.5)
    blk = jnp.arange(nb)[:, None, None] * BLOCK
    qpos = blk + jnp.arange(BLOCK)[None, :, None]
    kpos = blk - n_prev * BLOCK + jnp.arange((n_prev + 1) * BLOCK)[None, None, :]
    rel = qpos - kpos
    mask = (rel >= 0) & (rel < A_WINDOW) & (kpos >= 0)
    scores = jnp.where(mask[None, :, None, None], scores, -jnp.inf)
    sink = sinks.astype(jnp.float32).reshape(1, 1, g, r, 1, 1)
    m = jnp.maximum(jnp.max(scores, axis=-1, keepdims=True), sink)
    p = jnp.exp(scores - m)
    p = p / (jnp.sum(p, axis=-1, keepdims=True) + jnp.exp(sink - m))
    o = jnp.einsum('bngrqk,bnkgd->bnqgrd', p.astype(v.dtype), vc)
    return o.reshape(b, s, g * r * d)


def compress(t, pos_emb, w1, b1, w2):
    b, s, g, d = t.shape
    ratio = CMP_LEN // CMP_STRIDE
    chunks = t.reshape(b, s // CMP_STRIDE, CMP_STRIDE, g, d)
    nc = s // CMP_STRIDE - ratio + 1
    blocks = jnp.concatenate([chunks[:, j:j + nc] for j in range(ratio)], axis=2)
    blocks = blocks + pos_emb[None, None, :, None, :]
    flat = blocks.transpose(0, 1, 3, 2, 4).reshape(b, nc, g, CMP_LEN * d)
    hid = jax.nn.gelu(flat @ w1 + b1, approximate=True)
    return hid @ w2


def nsa_attention(q_raw, q_rot, k_cmp, v_cmp, k_slc, v_slc, k_win, v_win, gates):
    b, s, g, r, d = q_rot.shape
    nb = s // BLOCK
    nc = k_cmp.shape[1]
    ns = s // SLC_LEN
    n_sel = min(N_SLC, ns)
    scale = d ** -0.5
    ratio_c = CMP_LEN // CMP_STRIDE
    ratio_s = SLC_LEN // CMP_STRIDE
    ks_blocks = k_slc.reshape(b, ns, SLC_LEN, g, d).transpose(0, 3, 1, 2, 4)
    vs_blocks = v_slc.reshape(b, ns, SLC_LEN, g, d).transpose(0, 3, 1, 2, 4)
    kw_pad = jnp.pad(k_win, ((0, 0), (B_WINDOW, 0), (0, 0), (0, 0)))
    vw_pad = jnp.pad(v_win, ((0, 0), (B_WINDOW, 0), (0, 0), (0, 0)))
    cmp_end = jnp.arange(nc) * CMP_STRIDE + CMP_LEN - 1
    blk_ids = jnp.arange(ns)
    gather = jax.vmap(jax.vmap(lambda blocks, ix: blocks[ix]))

    def one_block(i):
        start = i * BLOCK
        t = start + jnp.arange(BLOCK)
        qr = lax.dynamic_slice_in_dim(q_rot, start, BLOCK, axis=1)
        qn = lax.dynamic_slice_in_dim(q_raw, start, BLOCK, axis=1)
        gt = lax.dynamic_slice_in_dim(gates, start, BLOCK, axis=1)
        s_c = jnp.einsum('bqgrd,bcgd->bgrqc', qn, k_cmp).astype(jnp.float32) * scale
        p_c = masked_softmax(s_c, cmp_end[None, :] <= t[:, None])
        o_c = jnp.einsum('bgrqc,bcgd->bqgrd', p_c.astype(v_cmp.dtype), v_cmp)
        imp = jnp.sum(p_c, axis=2)
        imp = jnp.pad(imp, ((0, 0), (0, 0), (0, 0), (ratio_c - 1, ratio_s * ns - nc)))
        imp_s = jnp.zeros(imp.shape[:-1] + (ns,), jnp.float32)
        for m in range(ratio_s):
            for n in range(ratio_c):
                off = m - n + ratio_c - 1
                imp_s = imp_s + imp[..., off:off + ratio_s * ns:ratio_s]
        cur = t // SLC_LEN
        valid = blk_ids[None, :] <= cur[:, None]
        forced = (blk_ids[None, :] == 0) | (valid & (blk_ids[None, :] > cur[:, None] - N_LOCAL_SLC))
        imp_s = jnp.where(forced, jnp.inf, jnp.where(valid, imp_s, -jnp.inf))
        _, idx = lax.top_k(imp_s, n_sel)
        k_sel = gather(ks_blocks, idx)
        v_sel = gather(vs_blocks, idx)
        s_s = jnp.einsum('bqgrd,bgqkld->bgrqkl', qr, k_sel).astype(jnp.float32) * scale
        tok = idx[..., None] * SLC_LEN + jnp.arange(SLC_LEN)
        mask_s = (tok <= t[None, None, :, None, None]).reshape(b, g, 1, BLOCK, n_sel * SLC_LEN)
        p_s = masked_softmax(s_s.reshape(b, g, r, BLOCK, n_sel * SLC_LEN), mask_s)
        o_s = jnp.einsum('bgrqn,bgqnd->bqgrd', p_s.astype(v_slc.dtype),
                         v_sel.reshape(b, g, BLOCK, n_sel * SLC_LEN, d))
        kwin = lax.dynamic_slice_in_dim(kw_pad, start, BLOCK + B_WINDOW, axis=1)
        vwin = lax.dynamic_slice_in_dim(vw_pad, start, BLOCK + B_WINDOW, axis=1)
        kpos = start - B_WINDOW + jnp.arange(BLOCK + B_WINDOW)
        rel = t[:, None] - kpos[None, :]
        mask_w = (rel >= 0) & (rel < B_WINDOW) & (kpos[None, :] >= 0)
        s_w = jnp.einsum('bqgrd,bkgd->bgrqk', qr, kwin).astype(jnp.float32) * scale
        p_w = masked_softmax(s_w, mask_w)
        o_w = jnp.einsum('bgrqk,bkgd->bqgrd', p_w.astype(v_win.dtype), vwin)
        return gt[..., 0:1] * o_c + gt[..., 1:2] * o_s + gt[..., 2:3] * o_w

    out = lax.map(one_block, jnp.arange(nb))
    return out.transpose(1, 0, 2, 3, 4, 5).reshape(b, s, g * r * d)


def conv_ffn(h, w_up, conv_w, conv_b, w_down):
    u = h @ w_up
    s = u.shape[1]
    up = jnp.pad(u, ((0, 0), (CONV_WIDTH - 1, 0), (0, 0)))
    c = conv_b
    for k in range(CONV_WIDTH):
        c = c + conv_w[k] * up[:, k:k + s]
    a, val = jnp.split(c, 2, axis=-1)
    return (jax.nn.gelu(a, approximate=True) * val) @ w_down


def hybrid_layer(x, positions, attn_pre_gain, attn_post_gain, ffn_pre_gain, ffn_post_gain,
                 w_in, attn_sinks, cmp_pos_emb, cmp_w1, cmp_b1, cmp_w2,
                 w_branch_a, w_branch_b, w_out, ffn_w_up, ffn_conv_w, ffn_conv_b, ffn_w_down):
    b, s, _ = x.shape
    hd = HEAD_DIM
    h = rms_norm(x, attn_pre_gain)
    u = h @ w_in
    offsets = np.cumsum(np.array(IN_SIZES))[:-1].tolist()
    (qa, ka, va, qb, kcb, vcb, ksb, vsb, kwb, vwb, g_nsa, g_a, g_b) = jnp.split(u, offsets, axis=-1)
    ra = A_Q_HEADS // A_KV_HEADS
    qa = apply_rope(qa.reshape(b, s, A_Q_HEADS, hd), positions).reshape(b, s, A_KV_HEADS, ra, hd)
    ka = apply_rope(ka.reshape(b, s, A_KV_HEADS, hd), positions)
    va = va.reshape(b, s, A_KV_HEADS, hd)
    o_a = swa_sink_attention(qa, ka, va, attn_sinks)
    rb = B_Q_HEADS // B_KV_HEADS
    qb = qb.reshape(b, s, B_Q_HEADS, hd)
    qb_rot = apply_rope(qb, positions).reshape(b, s, B_KV_HEADS, rb, hd)
    qb_raw = qb.reshape(b, s, B_KV_HEADS, rb, hd)
    kc = compress(kcb.reshape(b, s, B_KV_HEADS, hd), cmp_pos_emb[0], cmp_w1[0], cmp_b1[0], cmp_w2[0])
    vc = compress(vcb.reshape(b, s, B_KV_HEADS, hd), cmp_pos_emb[1], cmp_w1[1], cmp_b1[1], cmp_w2[1])
    ks = apply_rope(ksb.reshape(b, s, B_KV_HEADS, hd), positions)
    kw = apply_rope(kwb.reshape(b, s, B_KV_HEADS, hd), positions)
    vs = vsb.reshape(b, s, B_KV_HEADS, hd)
    vw = vwb.reshape(b, s, B_KV_HEADS, hd)
    gates = jax.nn.sigmoid(g_nsa).reshape(b, s, B_KV_HEADS, rb, 3)
    o_b = nsa_attention(qb_raw, qb_rot, kc, vc, ks, vs, kw, vw, gates)
    mixed = jax.nn.sigmoid(g_a) * (o_a @ w_branch_a) + jax.nn.sigmoid(g_b) * (o_b @ w_branch_b)
    x = x + rms_norm(mixed @ w_out, attn_post_gain)
    h = rms_norm(x, ffn_pre_gain)
    x = x + rms_norm(conv_ffn(h, ffn_w_up, ffn_conv_w, ffn_conv_b, ffn_w_down), ffn_post_gain)
    return x


def setup_inputs(seed: int = 0) -> dict:
    key = jax.random.key(seed)
    ks = jax.random.split(key, 20)
    f32 = jnp.float32

    def nrm(k, shape, scale):
        return scale * jax.random.normal(k, shape, f32)

    x = nrm(ks[0], (BATCH, SEQ, D_MODEL), 1.0)
    positions = jnp.broadcast_to(jnp.arange(SEQ, dtype=jnp.int32)[None, :], (BATCH, SEQ))
    return {
        'x': x,
        'positions': positions,
        'attn_pre_gain': 1.0 + nrm(ks[1], (DEPTH, D_MODEL), 0.05),
        'attn_post_gain': 1.0 + nrm(ks[2], (DEPTH, D_MODEL), 0.05),
        'ffn_pre_gain': 1.0 + nrm(ks[3], (DEPTH, D_MODEL), 0.05),
        'ffn_post_gain': 1.0 + nrm(ks[4], (DEPTH, D_MODEL), 0.05),
        'w_in': nrm(ks[5], (DEPTH, D_MODEL, IN_COLS), D_MODEL ** -0.5),
        'attn_sinks': nrm(ks[6], (DEPTH, A_Q_HEADS), 1.0),
        'cmp_pos_emb': nrm(ks[7], (DEPTH, 2, CMP_LEN, HEAD_DIM), 0.1),
        'cmp_w1': nrm(ks[8], (DEPTH, 2, CMP_LEN * HEAD_DIM, CMP_HIDDEN), (CMP_LEN * HEAD_DIM) ** -0.5),
        'cmp_b1': nrm(ks[9], (DEPTH, 2, CMP_HIDDEN), 0.01),
        'cmp_w2': nrm(ks[10], (DEPTH, 2, CMP_HIDDEN, HEAD_DIM), CMP_HIDDEN ** -0.5),
        'w_branch_a': nrm(ks[11], (DEPTH, A_WIDTH, D_MODEL), A_WIDTH ** -0.5),
        'w_branch_b': nrm(ks[12], (DEPTH, B_WIDTH, D_MODEL), B_WIDTH ** -0.5),
        'w_out': nrm(ks[13], (DEPTH, D_MODEL, D_MODEL), D_MODEL ** -0.5),
        'ffn_w_up': nrm(ks[14], (DEPTH, D_MODEL, 2 * D_FF), D_MODEL ** -0.5),
        'ffn_conv_w': nrm(ks[15], (DEPTH, CONV_WIDTH, 2 * D_FF), CONV_WIDTH ** -0.5),
        'ffn_conv_b': nrm(ks[16], (DEPTH, 2 * D_FF), 0.01),
        'ffn_w_down': nrm(ks[17], (DEPTH, D_FF, D_MODEL), D_FF ** -0.5),
    }


def reference(x, positions, attn_pre_gain, attn_post_gain, ffn_pre_gain, ffn_post_gain,
              w_in, attn_sinks, cmp_pos_emb, cmp_w1, cmp_b1, cmp_w2,
              w_branch_a, w_branch_b, w_out, ffn_w_up, ffn_conv_w, ffn_conv_b, ffn_w_down):
    for l in range(DEPTH):
        x = hybrid_layer(x, positions, attn_pre_gain[l], attn_post_gain[l], ffn_pre_gain[l],
                         ffn_post_gain[l], w_in[l], attn_sinks[l], cmp_pos_emb[l], cmp_w1[l],
                         cmp_b1[l], cmp_w2[l], w_branch_a[l], w_branch_b[l], w_out[l],
                         ffn_w_up[l], ffn_conv_w[l], ffn_conv_b[l], ffn_w_down[l])
    return x
```

```python
import functools

import numpy as np
import jax
import jax.numpy as jnp
from jax import lax
from jax.experimental import pallas as pl
from jax.experimental.pallas import tpu as pltpu

F32 = jnp.float32
BF16 = jnp.bfloat16

D_MODEL = 1024
HEAD_DIM = 64
BLOCK = 128
Q_HEADS = 8
KV_HEADS = 2
GROUP = Q_HEADS // KV_HEADS
A_WINDOW = 128
B_WINDOW = 512
CMP_LEN = 32
CMP_STRIDE = 16
CMP_HIDDEN = 128
SLC_LEN = 64
N_SLC = 16
N_LOCAL_SLC = 2
D_FF = 2816
ROPE_THETA = 10000.0
EPS = 1e-6
TINY = 1e-30
NEG_INF = float("-inf")
Q_SCALE = HEAD_DIM ** -0.5

LANES = 128
VMEM_LIMIT = 56 * 1024 * 1024

Q_WIDTH = Q_HEADS * HEAD_DIM
KV_WIDTH = KV_HEADS * HEAD_DIM
KV_DUP = 2 * KV_WIDTH
N_GATES = 3 * Q_HEADS

SEL_TILE = 512
FFN_CHUNK = 256
FFN_HALO = 16


def _params(sem):
    return pltpu.CompilerParams(dimension_semantics=sem, vmem_limit_bytes=VMEM_LIMIT)


def _rms(x, gain):
    return x * lax.rsqrt(jnp.mean(x * x, axis=-1, keepdims=True) + EPS) * gain


def _gelu_tanh(x):
    return 0.5 * x * (1.0 + jnp.tanh(0.7978845608028654 * (x + 0.044715 * (x * x * x))))


def _sigmoid(x):
    return 1.0 / (1.0 + jnp.exp(-x))


def _rope_table_kernel(pos_ref, invf_ref, cos_ref, sin_ref):
    ang = pos_ref[...].astype(F32) * invf_ref[...]
    lane = lax.broadcasted_iota(jnp.int32, ang.shape, 1)
    sign = jnp.where((lane & (HEAD_DIM - 1)) < HEAD_DIM // 2, -1.0, 1.0).astype(F32)
    cos_ref[...] = jnp.cos(ang)
    sin_ref[...] = jnp.sin(ang) * sign


def _rope_tables(positions):
    t = positions.size
    tm = min(1024, t)
    half = HEAD_DIM // 2
    inv_freq = ROPE_THETA ** (-jnp.arange(half, dtype=F32) / half)
    invf = jnp.tile(inv_freq, LANES // half).reshape(1, LANES)
    return pl.pallas_call(
        _rope_table_kernel,
        grid=(t // tm,),
        in_specs=[pl.BlockSpec((tm, 1), lambda i: (i, 0)),
                  pl.BlockSpec((1, LANES), lambda i: (0, 0))],
        out_specs=[pl.BlockSpec((tm, LANES), lambda i: (i, 0))] * 2,
        out_shape=[jax.ShapeDtypeStruct((t, LANES), F32)] * 2,
        compiler_params=_params(("parallel",)),
        name="rope_tables",
    )(positions.reshape(t, 1), invf)


def _rope(u, cos, sin):
    lane = lax.broadcasted_iota(jnp.int32, cos.shape, 1)
    first_half = (lane & (HEAD_DIM - 1)) < HEAD_DIM // 2
    outs = []
    for c in range(u.shape[1] // LANES):
        x = u[:, c * LANES:(c + 1) * LANES]
        partner = jnp.where(first_half,
                            pltpu.roll(x, LANES - HEAD_DIM // 2, axis=1),
                            pltpu.roll(x, HEAD_DIM // 2, axis=1))
        outs.append(x * cos + partner * sin)
    return outs[0] if len(outs) == 1 else jnp.concatenate(outs, axis=1)


_IN_SEGS = (("qa", Q_WIDTH), ("ka", KV_DUP), ("va", KV_DUP), ("qb", Q_WIDTH),
            ("kc", KV_WIDTH), ("vc", KV_WIDTH), ("ks", KV_DUP), ("vs", KV_DUP),
            ("kw", KV_DUP), ("vw", KV_DUP), ("gn", LANES), ("ga", D_MODEL), ("gb", D_MODEL))
_IN_COLS_BIG = sum(w for _, w in _IN_SEGS)


def _inproj_kernel(x_ref, gain_ref, w_ref, cos_ref, sin_ref,
                   qa_ref, ka_ref, va_ref, qbraw_ref, qbrot_ref, kc_ref, vc_ref,
                   ks_ref, vs_ref, kw_ref, vw_ref, gn_ref, ga_ref, gb_ref):
    h = _rms(x_ref[...], gain_ref[...]).astype(BF16)
    cos = cos_ref[...]
    sin = sin_ref[...]
    offs = {}
    off = 0
    for name, width in _IN_SEGS:
        offs[name] = (off, width)
        off += width

    def proj(name):
        o, w = offs[name]
        return jnp.dot(h, w_ref[:, o:o + w], preferred_element_type=F32)

    qa_ref[...] = (_rope(proj("qa"), cos, sin) * Q_SCALE).astype(BF16)
    ka_ref[...] = _rope(proj("ka"), cos, sin).astype(BF16)
    va_ref[...] = proj("va").astype(BF16)
    qb = proj("qb")
    qbraw_ref[...] = (qb * Q_SCALE).astype(BF16)
    qbrot_ref[...] = (_rope(qb, cos, sin) * Q_SCALE).astype(BF16)
    kc_ref[...] = proj("kc")
    vc_ref[...] = proj("vc")
    ks_ref[...] = _rope(proj("ks"), cos, sin).astype(BF16)
    vs_ref[...] = proj("vs").astype(BF16)
    kw_ref[...] = _rope(proj("kw"), cos, sin).astype(BF16)
    vw_ref[...] = proj("vw").astype(BF16)
    gn_ref[...] = _sigmoid(proj("gn"))
    ga_ref[...] = _sigmoid(proj("ga")).astype(BF16)
    gb_ref[...] = _sigmoid(proj("gb")).astype(BF16)


def _dup_heads(w):
    g0, g1 = w[:, :HEAD_DIM], w[:, HEAD_DIM:]
    return jnp.concatenate([g0, g0, g1, g1], axis=1)


def _inproj_weight(w_in):
    sizes = (Q_WIDTH, KV_WIDTH, KV_WIDTH, Q_WIDTH) + (KV_WIDTH,) * 6 + (N_GATES, D_MODEL, D_MODEL)
    offs = np.cumsum(sizes)[:-1].tolist()
    qa, ka, va, qb, kc, vc, ks, vs, kw, vw, gn, ga, gb = jnp.split(w_in, offs, axis=1)
    gn = jnp.pad(gn, ((0, 0), (0, LANES - N_GATES)))
    cols = [qa, _dup_heads(ka), _dup_heads(va), qb, kc, vc, _dup_heads(ks), _dup_heads(vs),
            _dup_heads(kw), _dup_heads(vw), gn, ga, gb]
    return jnp.concatenate(cols, axis=1).astype(BF16)


def _inproj(x2, gain, w_big, cos, sin):
    t = x2.shape[0]
    tm = 256
    row = lambda w: pl.BlockSpec((tm, w), lambda i: (i, 0))
    fixed = lambda r, c: pl.BlockSpec((r, c), lambda i: (0, 0))
    out_dt = [("qa", Q_WIDTH, BF16), ("ka", KV_DUP, BF16), ("va", KV_DUP, BF16),
              ("qbraw", Q_WIDTH, BF16), ("qbrot", Q_WIDTH, BF16),
              ("kc", KV_WIDTH, F32), ("vc", KV_WIDTH, F32),
              ("ks", KV_DUP, BF16), ("vs", KV_DUP, BF16), ("kw", KV_DUP, BF16), ("vw", KV_DUP, BF16),
              ("gn", LANES, F32), ("ga", D_MODEL, BF16), ("gb", D_MODEL, BF16)]
    outs = pl.pallas_call(
        _inproj_kernel,
        grid=(t // tm,),
        in_specs=[row(D_MODEL), fixed(1, D_MODEL), fixed(D_MODEL, _IN_COLS_BIG), row(LANES), row(LANES)],
        out_specs=[row(w) for _, w, _ in out_dt],
        out_shape=[jax.ShapeDtypeStruct((t, w), dt) for _, w, dt in out_dt],
        compiler_params=_params(("parallel",)),
        name="in_proj",
    )(x2, gain.reshape(1, D_MODEL), w_big, cos, sin)
    return {name: o for (name, _, _), o in zip(out_dt, outs)}


def _compress_kernel(flat_ref, pe_ref, w1t_ref, w1b_ref, b1_ref, w2_ref, o_ref):
    f = flat_ref[...]
    pe = pe_ref[...]
    top = jnp.dot((f + pe[0:1]).astype(BF16), w1t_ref[...], preferred_element_type=F32)
    bot = jnp.dot((f + pe[1:2]).astype(BF16), w1b_ref[...], preferred_element_type=F32)
    ncp = f.shape[0]
    hid = _gelu_tanh(top + pltpu.roll(bot, ncp - 1, axis=0) + b1_ref[...])
    o_ref[...] = jnp.dot(hid.astype(BF16), w2_ref[...], preferred_element_type=F32).astype(BF16)


def _compress_weights(pos_emb, w1, b1, w2):
    half = CMP_LEN // 2
    eye = jnp.eye(KV_HEADS, dtype=F32)
    w1r = w1.reshape(2, CMP_LEN, HEAD_DIM, CMP_HIDDEN)

    def big(part):
        return jnp.einsum("wtdh,gk->wtgdkh", part, eye).reshape(
            2, half * KV_WIDTH, KV_HEADS * CMP_HIDDEN).astype(BF16)

    def pe_big(part):
        return jnp.broadcast_to(part[:, :, None, :], (2, half, KV_HEADS, HEAD_DIM)).reshape(2, half * KV_WIDTH)

    pe = jnp.stack([pe_big(pos_emb[:, :half]), pe_big(pos_emb[:, half:])], axis=1)
    b1b = jnp.tile(b1, (1, KV_HEADS)).reshape(2, 1, KV_HEADS * CMP_HIDDEN)
    w2b = jnp.einsum("whd,gk,r->wghkrd", w2, eye, jnp.ones((2,), F32)).reshape(
        2, KV_HEADS * CMP_HIDDEN, KV_DUP).astype(BF16)
    return pe, big(w1r[:, :half]), big(w1r[:, half:]), b1b, w2b


def _compress(flat, cw):
    pe, w1t, w1b, b1b, w2b = cw
    _, b, ncp, fw = flat.shape
    hw = KV_HEADS * CMP_HIDDEN
    per_w = lambda *shape: pl.BlockSpec((None,) + shape, lambda w, bb: (w,) + (0,) * len(shape))
    return pl.pallas_call(
        _compress_kernel,
        grid=(2, b),
        in_specs=[pl.BlockSpec((None, None, ncp, fw), lambda w, bb: (w, bb, 0, 0)),
                  per_w(2, fw), per_w(fw, hw), per_w(fw, hw), per_w(1, hw), per_w(hw, KV_DUP)],
        out_specs=pl.BlockSpec((None, None, ncp, KV_DUP), lambda w, bb: (w, bb, 0, 0)),
        out_shape=jax.ShapeDtypeStruct((2, b, ncp, KV_DUP), BF16),
        compiler_params=_params(("parallel", "parallel")),
        name="compress",
    )(flat, pe, w1t, w1b, b1b, w2b)


def _stack_heads(q):
    lane = lax.broadcasted_iota(jnp.int32, (BLOCK, LANES), 1)
    lo = lane < HEAD_DIM
    zero = jnp.zeros((BLOCK, LANES), q.dtype)
    p0, p1 = q[:, :LANES], q[:, LANES:]
    return jnp.concatenate([jnp.where(lo, p0, zero), jnp.where(lo, zero, p0),
                            jnp.where(lo, p1, zero), jnp.where(lo, zero, p1)], axis=0)


def _unstack_out(o_t):
    lane = lax.broadcasted_iota(jnp.int32, (BLOCK, LANES), 1)
    lo = lane < HEAD_DIM
    t = [o_t[:, h * BLOCK:(h + 1) * BLOCK].T for h in range(GROUP)]
    return jnp.concatenate([jnp.where(lo, t[0], t[1]), jnp.where(lo, t[2], t[3])], axis=1)


def _gate_row(g_ref, branch):
    g = g_ref[...]
    return jnp.concatenate([g[branch * GROUP + h:branch * GROUP + h + 1, :] for h in range(GROUP)], axis=1)


def _nt_dot(a, b):
    return lax.dot_general(a, b, (((1,), (1,)), ((), ())), preferred_element_type=F32)


def _attn_specs(s):
    nb = s // BLOCK
    q_spec = pl.BlockSpec((None, BLOCK, GROUP * HEAD_DIM), lambda b, g, i: (b, i, g))
    k_spec = pl.BlockSpec((None, s, LANES), lambda b, g, i: (b, 0, g))
    vt_spec = pl.BlockSpec((None, LANES, s), lambda b, g, i: (b, g, 0))
    gate_spec = pl.BlockSpec((None, None, 16, BLOCK), lambda b, g, i: (b, g, 0, i))
    return nb, q_spec, k_spec, vt_spec, gate_spec


def _band_kernel(*refs, window, nkb, use_sink, gate_branch, has_prev):
    refs = list(refs)
    q_ref, k_ref, vt_ref = refs[:3]
    pos = 3
    g_ref = sink_ref = prev_ref = None
    if gate_branch is not None:
        g_ref = refs[pos]; pos += 1
    if use_sink:
        sink_ref = refs[pos]; pos += 1
    if has_prev:
        prev_ref = refs[pos]; pos += 1
    o_ref = refs[pos]

    i = pl.program_id(2)
    length = nkb * BLOCK
    start = pl.multiple_of(jnp.maximum(i - (nkb - 1), 0) * BLOCK, BLOCK)
    kb = k_ref[pl.ds(start, length), :]
    vb = vt_ref[:, pl.ds(start, length)]
    qs = _stack_heads(q_ref[...])
    s = _nt_dot(kb, qs)
    kpos = start + lax.broadcasted_iota(jnp.int32, s.shape, 0)
    qpos = i * BLOCK + (lax.broadcasted_iota(jnp.int32, s.shape, 1) & (BLOCK - 1))
    rel = qpos - kpos
    s = jnp.where((rel >= 0) & (rel < window), s, NEG_INF)
    m = jnp.max(s, axis=0, keepdims=True)
    if use_sink:
        sink = sink_ref[...]
        m = jnp.maximum(m, sink)
    p = jnp.exp(s - m)
    l = jnp.sum(p, axis=0, keepdims=True)
    if use_sink:
        l = l + jnp.exp(sink - m)
    o_t = jnp.dot(vb, p.astype(BF16), preferred_element_type=F32)
    scale = 1.0 / jnp.maximum(l, TINY)
    if gate_branch is not None:
        scale = scale * _gate_row(g_ref, gate_branch)
    out = _unstack_out(o_t * scale)
    if has_prev:
        out = out + prev_ref[...]
    o_ref[...] = out.astype(o_ref.dtype)


def _band_attention(q, k, vt, *, window, gates=None, gate_branch=None, sinks=None, prev=None, out_dtype=F32):
    b, s, _ = q.shape
    nb, q_spec, k_spec, vt_spec, gate_spec = _attn_specs(s)
    nkb = -(-window // BLOCK) + 1
    ins, specs = [q, k, vt], [q_spec, k_spec, vt_spec]
    if gate_branch is not None:
        ins.append(gates); specs.append(gate_spec)
    if sinks is not None:
        ins.append(sinks)
        specs.append(pl.BlockSpec((None, 1, GROUP * BLOCK), lambda b_, g, i: (g, 0, 0)))
    if prev is not None:
        ins.append(prev); specs.append(q_spec)
    kern = functools.partial(_band_kernel, window=window, nkb=nkb, use_sink=sinks is not None,
                             gate_branch=gate_branch, has_prev=prev is not None)
    return pl.pallas_call(
        kern,
        grid=(b, KV_HEADS, nb),
        in_specs=specs,
        out_specs=q_spec,
        out_shape=jax.ShapeDtypeStruct((b, s, Q_WIDTH), out_dtype),
        compiler_params=_params(("parallel", "parallel", "parallel")),
        name=f"band_attn_w{window}",
    )(*ins)


def _cmp_kernel(q_ref, kc_ref, vct_ref, g_ref, mt_ref, o_ref, sel_ref, *, n_sel):
    i = pl.program_id(2)
    qs = _stack_heads(q_ref[...])
    s = _nt_dot(kc_ref[...], qs)
    c = lax.broadcasted_iota(jnp.int32, s.shape, 0)
    t = i * BLOCK + (lax.broadcasted_iota(jnp.int32, s.shape, 1) & (BLOCK - 1))
    s = jnp.where(c * CMP_STRIDE + (CMP_LEN - 1) <= t, s, NEG_INF)
    m = jnp.max(s, axis=0, keepdims=True)
    m = jnp.where(m == NEG_INF, 0.0, m)
    p = jnp.exp(s - m)
    l = jnp.sum(p, axis=0, keepdims=True)
    p = p * (1.0 / jnp.maximum(l, TINY))
    o_t = jnp.dot(vct_ref[...], p.astype(BF16), preferred_element_type=F32)
    o_ref[...] = _unstack_out(o_t * _gate_row(g_ref, 0))

    imp = p[:, 0:BLOCK]
    for h in range(1, GROUP):
        imp = imp + p[:, h * BLOCK:(h + 1) * BLOCK]
    hi = imp.astype(BF16)
    r1 = imp - hi.astype(F32)
    mid = r1.astype(BF16)
    lo = (r1 - mid.astype(F32)).astype(BF16)
    mt = mt_ref[...]
    imp_s = (jnp.dot(mt, hi, preferred_element_type=F32) + jnp.dot(mt, mid, preferred_element_type=F32)
             + jnp.dot(mt, lo, preferred_element_type=F32))

    ns = imp_s.shape[0]
    j = lax.broadcasted_iota(jnp.int32, imp_s.shape, 0)
    tq = i * BLOCK + lax.broadcasted_iota(jnp.int32, imp_s.shape, 1)
    cur = tq >> (SLC_LEN.bit_length() - 1)
    valid = j <= cur
    forced = (j == 0) | (valid & (j > cur - N_LOCAL_SLC))
    work = jnp.where(forced, jnp.inf, jnp.where(valid, imp_s, NEG_INF))
    sel_bias = jnp.full(imp_s.shape, NEG_INF, F32)
    for _ in range(n_sel):
        mx = jnp.max(work, axis=0, keepdims=True)
        first = jnp.min(jnp.where(work == mx, j, ns), axis=0, keepdims=True)
        pick = j == first
        sel_bias = jnp.where(pick, 0.0, sel_bias)
        work = jnp.where(pick, NEG_INF, work)
    sel_ref[...] = sel_bias


def _overlap_map(ns, ncp):
    ratio_c = CMP_LEN // CMP_STRIDE
    ratio_s = SLC_LEN // CMP_STRIDE
    m = np.zeros((ns, ncp), np.float32)
    for jj in range(ns):
        for a in range(ratio_s):
            for n in range(ratio_c):
                c = ratio_s * jj + a - n
                if 0 <= c < ncp - ratio_c + 1:
                    m[jj, c] += 1.0
    return jnp.asarray(m, BF16)


def _cmp_attention(q_raw, kc, vct, gates):
    b, s, _ = q_raw.shape
    nb, q_spec, _, _, gate_spec = _attn_specs(s)
    ncp = s // CMP_STRIDE
    ns = s // SLC_LEN
    mt = _overlap_map(ns, ncp)
    kern = functools.partial(_cmp_kernel, n_sel=min(N_SLC, ns))
    return pl.pallas_call(
        kern,
        grid=(b, KV_HEADS, nb),
        in_specs=[q_spec,
                  pl.BlockSpec((None, ncp, LANES), lambda b_, g, i: (b_, 0, g)),
                  pl.BlockSpec((None, LANES, ncp), lambda b_, g, i: (b_, g, 0)),
                  gate_spec,
                  pl.BlockSpec((ns, ncp), lambda b_, g, i: (0, 0))],
        out_specs=[q_spec, pl.BlockSpec((None, None, ns, BLOCK), lambda b_, g, i: (b_, g, 0, i))],
        out_shape=[jax.ShapeDtypeStruct((b, s, Q_WIDTH), F32),
                   jax.ShapeDtypeStruct((b, KV_HEADS, ns, s), F32)],
        compiler_params=_params(("parallel", "parallel", "parallel")),
        name="cmp_attn_topk",
    )(q_raw, kc, vct, gates, mt)


def _sel_kernel(q_ref, k_ref, vt_ref, sel_ref, g_ref, prev_ref, o_ref, m_sc, l_sc, acc_sc):
    i = pl.program_id(2)
    tk = SEL_TILE
    nblk = tk // SLC_LEN
    qs = _stack_heads(q_ref[...])
    m_sc[...] = jnp.full(m_sc.shape, NEG_INF, F32)
    l_sc[...] = jnp.zeros(l_sc.shape, F32)
    acc_sc[...] = jnp.zeros(acc_sc.shape, F32)
    qpos = i * BLOCK + lax.broadcasted_iota(jnp.int32, (tk, BLOCK), 1)
    n_tiles = (i * BLOCK + BLOCK + tk - 1) // tk

    def body(jt, carry):
        k0 = pl.multiple_of(jt * tk, tk)
        s = _nt_dot(k_ref[pl.ds(k0, tk), :], qs)
        rows = sel_ref[pl.ds(pl.multiple_of(jt * nblk, nblk), nblk), :]
        bias = jnp.concatenate(
            [jnp.broadcast_to(rows[bb:bb + 1, :], (SLC_LEN, BLOCK)) for bb in range(nblk)], axis=0)
        kpos = k0 + lax.broadcasted_iota(jnp.int32, (tk, BLOCK), 0)
        bias = jnp.where(kpos <= qpos, bias, NEG_INF)
        s = s + jnp.concatenate([bias] * GROUP, axis=1)
        m_old = m_sc[...]
        m_new = jnp.maximum(m_old, jnp.max(s, axis=0, keepdims=True))
        m_safe = jnp.where(m_new == NEG_INF, 0.0, m_new)
        alpha = jnp.exp(m_old - m_safe)
        p = jnp.exp(s - m_safe)
        l_sc[...] = alpha * l_sc[...] + jnp.sum(p, axis=0, keepdims=True)
        acc_sc[...] = alpha * acc_sc[...] + jnp.dot(vt_ref[:, pl.ds(k0, tk)], p.astype(BF16),
                                                    preferred_element_type=F32)
        m_sc[...] = m_new
        return carry

    lax.fori_loop(0, n_tiles, body, 0)
    scale = _gate_row(g_ref, 1) / jnp.maximum(l_sc[...], TINY)
    o_ref[...] = _unstack_out(acc_sc[...] * scale) + prev_ref[...]


def _sel_attention(q_rot, ks, vst, sel, gates, prev):
    b, s, _ = q_rot.shape
    nb, q_spec, k_spec, vt_spec, gate_spec = _attn_specs(s)
    ns = s // SLC_LEN
    return pl.pallas_call(
        _sel_kernel,
        grid=(b, KV_HEADS, nb),
        in_specs=[q_spec, k_spec, vt_spec,
                  pl.BlockSpec((None, None, ns, BLOCK), lambda b_, g, i: (b_, g, 0, i)),
                  gate_spec, q_spec],
        out_specs=q_spec,
        out_shape=jax.ShapeDtypeStruct((b, s, Q_WIDTH), F32),
        scratch_shapes=[pltpu.VMEM((1, GROUP * BLOCK), F32), pltpu.VMEM((1, GROUP * BLOCK), F32),
                        pltpu.VMEM((LANES, GROUP * BLOCK), F32)],
        compiler_params=_params(("parallel", "parallel", "parallel")),
        name="sel_attn",
    )(q_rot, ks, vst, sel, gates, prev)


def _merge_kernel(oa_ref, ob_ref, ga_ref, gb_ref, x_ref, wa_ref, wb_ref, wo_ref, gain_ref, o_ref):
    ya = jnp.dot(oa_ref[...], wa_ref[...], preferred_element_type=F32)
    yb = jnp.dot(ob_ref[...].astype(BF16), wb_ref[...], preferred_element_type=F32)
    mixed = ga_ref[...].astype(F32) * ya + gb_ref[...].astype(F32) * yb
    z = jnp.dot(mixed.astype(BF16), wo_ref[...], preferred_element_type=F32)
    o_ref[...] = x_ref[...] + _rms(z, gain_ref[...])


def _merge(o_a, o_b, ga, gb, x2, w_a, w_b, w_o, gain):
    t = x2.shape[0]
    tm = 512
    row = lambda w: pl.BlockSpec((tm, w), lambda i: (i, 0))
    fixed = lambda r, c: pl.BlockSpec((r, c), lambda i: (0, 0))
    return pl.pallas_call(
        _merge_kernel,
        grid=(t // tm,),
        in_specs=[row(Q_WIDTH), row(Q_WIDTH), row(D_MODEL), row(D_MODEL), row(D_MODEL),
                  fixed(Q_WIDTH, D_MODEL), fixed(Q_WIDTH, D_MODEL), fixed(D_MODEL, D_MODEL), fixed(1, D_MODEL)],
        out_specs=row(D_MODEL),
        out_shape=jax.ShapeDtypeStruct((t, D_MODEL), F32),
        compiler_params=_params(("parallel",)),
        name="merge_out_proj",
    )(o_a, o_b, ga, gb, x2, w_a.astype(BF16), w_b.astype(BF16), w_o.astype(BF16), gain.reshape(1, D_MODEL))


def _ffn_kernel(xp_ref, x_ref, gpre_ref, wua_ref, wuv_ref, cwa_ref, cwv_ref, cba_ref, cbv_ref, wd_ref,
                gpost_ref, o_ref, h_sc, ua_sc, uv_sc, acc_sc, *, tm, tiles_per_seq):
    i = pl.program_id(0)
    n = pl.program_id(1)

    @pl.when(n == 0)
    def _():
        gpre = gpre_ref[...]
        keep = (i % tiles_per_seq != 0).astype(F32)
        h_sc[0:FFN_HALO, :] = (_rms(xp_ref[...], gpre) * keep).astype(BF16)
        h_sc[FFN_HALO:, :] = _rms(x_ref[...], gpre).astype(BF16)
        acc_sc[...] = jnp.zeros(acc_sc.shape, F32)

    h = h_sc[...]

    def conv(wu_ref, cw_ref, cb_ref, u_sc):
        u_sc[...] = jnp.dot(h, wu_ref[...], preferred_element_type=F32)
        cw = cw_ref[...]
        return (cb_ref[...] + cw[0:1] * u_sc[pl.ds(FFN_HALO - 2, tm), :]
                + cw[1:2] * u_sc[pl.ds(FFN_HALO - 1, tm), :] + cw[2:3] * u_sc[pl.ds(FFN_HALO, tm), :])

    a = conv(wua_ref, cwa_ref, cba_ref, ua_sc)
    v = conv(wuv_ref, cwv_ref, cbv_ref, uv_sc)
    acc_sc[...] += jnp.dot((_gelu_tanh(a) * v).astype(BF16), wd_ref[...], preferred_element_type=F32)

    @pl.when(n == pl.num_programs(1) - 1)
    def _():
        o_ref[...] = x_ref[...] + _rms(acc_sc[...], gpost_ref[...])


def _ffn(x2, seq, gpre, w_up, conv_w, conv_b, w_down, gpost):
    t = x2.shape[0]
    tm = min(1024, seq)
    ck = FFN_CHUNK
    nck = D_FF // ck
    kern = functools.partial(_ffn_kernel, tm=tm, tiles_per_seq=seq // tm)
    halo_blocks = tm // FFN_HALO
    return pl.pallas_call(
        kern,
        grid=(t // tm, nck),
        in_specs=[pl.BlockSpec((FFN_HALO, D_MODEL), lambda i, n: (jnp.maximum(i * halo_blocks - 1, 0), 0)),
                  pl.BlockSpec((tm, D_MODEL), lambda i, n: (i, 0)),
                  pl.BlockSpec((1, D_MODEL), lambda i, n: (0, 0)),
                  pl.BlockSpec((D_MODEL, ck), lambda i, n: (0, n)),
                  pl.BlockSpec((D_MODEL, ck), lambda i, n: (0, nck + n)),
                  pl.BlockSpec((3, ck), lambda i, n: (0, n)),
                  pl.BlockSpec((3, ck), lambda i, n: (0, nck + n)),
                  pl.BlockSpec((1, ck), lambda i, n: (0, n)),
                  pl.BlockSpec((1, ck), lambda i, n: (0, nck + n)),
                  pl.BlockSpec((ck, D_MODEL), lambda i, n: (n, 0)),
                  pl.BlockSpec((1, D_MODEL), lambda i, n: (0, 0))],
        out_specs=pl.BlockSpec((tm, D_MODEL), lambda i, n: (i, 0)),
        out_shape=jax.ShapeDtypeStruct((t, D_MODEL), F32),
        scratch_shapes=[pltpu.VMEM((tm + FFN_HALO, D_MODEL), BF16),
                        pltpu.VMEM((tm + FFN_HALO, ck), F32),
                        pltpu.VMEM((tm + FFN_HALO, ck), F32),
                        pltpu.VMEM((tm, D_MODEL), F32)],
        compiler_params=_params(("parallel", "arbitrary")),
        name="conv_ffn",
    )(x2, x2, gpre.reshape(1, D_MODEL), w_up.astype(BF16), w_up.astype(BF16), conv_w, conv_w,
      conv_b.reshape(1, 2 * D_FF), conv_b.reshape(1, 2 * D_FF), w_down.astype(BF16), gpost.reshape(1, D_MODEL))


def _to_t(v, b, s):
    return jnp.transpose(v.reshape(b, s, KV_DUP), (0, 2, 1))


def _layer(x2, b, s, cos, sin, p):
    t = b * s
    u = _inproj(x2, p["attn_pre_gain"], _inproj_weight(p["w_in"]), cos, sin)
    r3 = lambda a: a.reshape(b, s, a.shape[-1])

    gn = u["gn"][:, :N_GATES].reshape(b, s, KV_HEADS, GROUP, 3)
    gates = jnp.transpose(gn, (0, 2, 4, 3, 1)).reshape(b, KV_HEADS, 3 * GROUP, s)
    gates = jnp.pad(gates, ((0, 0), (0, 0), (0, 16 - 3 * GROUP), (0, 0)))
    sinks = jnp.repeat(p["attn_sinks"].astype(F32), BLOCK).reshape(KV_HEADS, 1, GROUP * BLOCK)

    o_a = _band_attention(r3(u["qa"]), r3(u["ka"]), _to_t(u["va"], b, s), window=A_WINDOW,
                          sinks=sinks, out_dtype=BF16)

    ncp = s // CMP_STRIDE
    flat = jnp.stack([u["kc"], u["vc"]]).reshape(2, b, ncp, CMP_STRIDE * KV_WIDTH)
    cmp = _compress(flat, _compress_weights(p["cmp_pos_emb"], p["cmp_w1"], p["cmp_b1"], p["cmp_w2"]))
    o_c, sel = _cmp_attention(r3(u["qbraw"]), cmp[0], jnp.transpose(cmp[1], (0, 2, 1)), gates)
    o_cs = _sel_attention(r3(u["qbrot"]), r3(u["ks"]), _to_t(u["vs"], b, s), sel, gates, o_c)
    o_b = _band_attention(r3(u["qbrot"]), r3(u["kw"]), _to_t(u["vw"], b, s), window=B_WINDOW,
                          gates=gates, gate_branch=2, prev=o_cs)

    x2 = _merge(o_a.reshape(t, Q_WIDTH), o_b.reshape(t, Q_WIDTH), u["ga"], u["gb"], x2,
                p["w_branch_a"], p["w_branch_b"], p["w_out"], p["attn_post_gain"])
    return _ffn(x2, s, p["ffn_pre_gain"], p["ffn_w_up"], p["ffn_conv_w"], p["ffn_conv_b"], p["ffn_w_down"],
                p["ffn_post_gain"])


def kernel(x, positions, attn_pre_gain, attn_post_gain, ffn_pre_gain, ffn_post_gain, w_in, attn_sinks,
           cmp_pos_emb, cmp_w1, cmp_b1, cmp_w2, w_branch_a, w_branch_b, w_out, ffn_w_up, ffn_conv_w,
           ffn_conv_b, ffn_w_down):
    b, s, d = x.shape
    params = dict(attn_pre_gain=attn_pre_gain, attn_post_gain=attn_post_gain, ffn_pre_gain=ffn_pre_gain,
                  ffn_post_gain=ffn_post_gain, w_in=w_in, attn_sinks=attn_sinks, cmp_pos_emb=cmp_pos_emb,
                  cmp_w1=cmp_w1, cmp_b1=cmp_b1, cmp_w2=cmp_w2, w_branch_a=w_branch_a, w_branch_b=w_branch_b,
                  w_out=w_out, ffn_w_up=ffn_w_up, ffn_conv_w=ffn_conv_w, ffn_conv_b=ffn_conv_b,
                  ffn_w_down=ffn_w_down)
    cos, sin = _rope_tables(positions)
    x2 = x.reshape(b * s, d)
    for layer in range(attn_pre_gain.shape[0]):
        x2 = _layer(x2, b, s, cos, sin, {k: v[layer] for k, v in params.items()})
    return x2.reshape(b, s, d)
```

```python
import functools

import numpy as np
import jax
import jax.numpy as jnp
from jax import lax
from jax.experimental import pallas as pl
from jax.experimental.pallas import tpu as pltpu

F32 = jnp.float32
BF16 = jnp.bfloat16

D_MODEL = 1024
HEAD_DIM = 64
BLOCK = 128
Q_HEADS = 8
KV_HEADS = 2
GROUP = Q_HEADS // KV_HEADS
A_WINDOW = 128
B_WINDOW = 512
CMP_LEN = 32
CMP_STRIDE = 16
CMP_HIDDEN = 128
SLC_LEN = 64
N_SLC = 16
N_LOCAL_SLC = 2
D_FF = 2816
ROPE_THETA = 10000.0
EPS = 1e-6
TINY = 1e-30
NEG_INF = float("-inf")
LOG2E = 1.4426950408889634
Q_SCALE = HEAD_DIM ** -0.5 * LOG2E
NEG_BIG = -1e30

LANES = 128
VMEM_LIMIT = 56 * 1024 * 1024

Q_WIDTH = Q_HEADS * HEAD_DIM
KV_WIDTH = KV_HEADS * HEAD_DIM
KV_DUP = 2 * KV_WIDTH
N_GATES = 3 * Q_HEADS

SEL_TILE = 512
FFN_CHUNK = 256
FFN_HALO = 16


def _params(sem):
    return pltpu.CompilerParams(dimension_semantics=sem, vmem_limit_bytes=VMEM_LIMIT)


def _rms(x, gain):
    return x * lax.rsqrt(jnp.mean(x * x, axis=-1, keepdims=True) + EPS) * gain


def _gelu_tanh(x):
    return 0.5 * x * (1.0 + jnp.tanh(0.7978845608028654 * (x + 0.044715 * (x * x * x))))


def _sigmoid(x):
    return 1.0 / (1.0 + jnp.exp(-x))


def _rope_table_kernel(pos_ref, invf_ref, cos_ref, sin_ref):
    ang = pos_ref[...].astype(F32) * invf_ref[...]
    lane = lax.broadcasted_iota(jnp.int32, ang.shape, 1)
    sign = jnp.where((lane & (HEAD_DIM - 1)) < HEAD_DIM // 2, -1.0, 1.0).astype(F32)
    cos_ref[...] = jnp.cos(ang)
    sin_ref[...] = jnp.sin(ang) * sign


def _rope_tables(positions):
    t = positions.size
    tm = min(1024, t)
    half = HEAD_DIM // 2
    inv_freq = ROPE_THETA ** (-jnp.arange(half, dtype=F32) / half)
    invf = jnp.tile(inv_freq, LANES // half).reshape(1, LANES)
    return pl.pallas_call(
        _rope_table_kernel,
        grid=(t // tm,),
        in_specs=[pl.BlockSpec((tm, 1), lambda i: (i, 0)),
                  pl.BlockSpec((1, LANES), lambda i: (0, 0))],
        out_specs=[pl.BlockSpec((tm, LANES), lambda i: (i, 0))] * 2,
        out_shape=[jax.ShapeDtypeStruct((t, LANES), F32)] * 2,
        compiler_params=_params(("parallel",)),
        name="rope_tables",
    )(positions.reshape(t, 1), invf)


def _rope(u, cos, sin):
    lane = lax.broadcasted_iota(jnp.int32, cos.shape, 1)
    first_half = (lane & (HEAD_DIM - 1)) < HEAD_DIM // 2
    outs = []
    for c in range(u.shape[1] // LANES):
        x = u[:, c * LANES:(c + 1) * LANES]
        partner = jnp.where(first_half,
                            pltpu.roll(x, LANES - HEAD_DIM // 2, axis=1),
                            pltpu.roll(x, HEAD_DIM // 2, axis=1))
        outs.append(x * cos + partner * sin)
    return outs[0] if len(outs) == 1 else jnp.concatenate(outs, axis=1)


_IN_SEGS = (("qa", Q_WIDTH), ("ka", KV_DUP), ("va", KV_WIDTH), ("qb", Q_WIDTH),
            ("kc", KV_WIDTH), ("vc", KV_WIDTH), ("ks", KV_DUP), ("vs", KV_WIDTH),
            ("kw", KV_DUP), ("vw", KV_WIDTH), ("gn", LANES), ("ga", D_MODEL), ("gb", D_MODEL))
_IN_COLS_BIG = sum(w for _, w in _IN_SEGS)


def _inproj_kernel(x_ref, gain_ref, w_ref, cos_ref, sin_ref,
                   qa_ref, ka_ref, va_ref, qbraw_ref, qbrot_ref, kc_ref, vc_ref,
                   ks_ref, vs_ref, kw_ref, vw_ref, gn_ref, ga_ref, gb_ref):
    h = _rms(x_ref[...], gain_ref[...]).astype(BF16)
    cos = cos_ref[...]
    sin = sin_ref[...]
    offs = {}
    off = 0
    for name, width in _IN_SEGS:
        offs[name] = (off, width)
        off += width

    def proj(name):
        o, w = offs[name]
        return jnp.dot(h, w_ref[:, o:o + w], preferred_element_type=F32)

    qa_ref[...] = (_rope(proj("qa"), cos, sin) * Q_SCALE).astype(BF16)
    ka_ref[...] = _rope(proj("ka"), cos, sin).astype(BF16)
    va_ref[...] = proj("va").astype(BF16)
    qb = proj("qb")
    qbraw_ref[...] = (qb * Q_SCALE).astype(BF16)
    qbrot_ref[...] = (_rope(qb, cos, sin) * Q_SCALE).astype(BF16)
    kc_ref[...] = proj("kc")
    vc_ref[...] = proj("vc")
    ks_ref[...] = _rope(proj("ks"), cos, sin).astype(BF16)
    vs_ref[...] = proj("vs").astype(BF16)
    kw_ref[...] = _rope(proj("kw"), cos, sin).astype(BF16)
    vw_ref[...] = proj("vw").astype(BF16)
    gn_ref[...] = _sigmoid(proj("gn"))
    ga_ref[...] = _sigmoid(proj("ga")).astype(BF16)
    gb_ref[...] = _sigmoid(proj("gb")).astype(BF16)


def _dup_heads(w):
    g0, g1 = w[:, :HEAD_DIM], w[:, HEAD_DIM:]
    return jnp.concatenate([g0, g0, g1, g1], axis=1)


def _inproj_weight(w_in):
    sizes = (Q_WIDTH, KV_WIDTH, KV_WIDTH, Q_WIDTH) + (KV_WIDTH,) * 6 + (N_GATES, D_MODEL, D_MODEL)
    offs = np.cumsum(sizes)[:-1].tolist()
    qa, ka, va, qb, kc, vc, ks, vs, kw, vw, gn, ga, gb = jnp.split(w_in, offs, axis=1)
    gn = jnp.pad(gn, ((0, 0), (0, LANES - N_GATES)))
    cols = [qa, _dup_heads(ka), va, qb, kc, vc, _dup_heads(ks), vs, _dup_heads(kw), vw, gn, ga, gb]
    return jnp.concatenate(cols, axis=1).astype(BF16)


def _inproj(x2, gain, w_big, cos, sin):
    t = x2.shape[0]
    tm = 256
    row = lambda w: pl.BlockSpec((tm, w), lambda i: (i, 0))
    fixed = lambda r, c: pl.BlockSpec((r, c), lambda i: (0, 0))
    out_dt = [("qa", Q_WIDTH, BF16), ("ka", KV_DUP, BF16), ("va", KV_WIDTH, BF16),
              ("qbraw", Q_WIDTH, BF16), ("qbrot", Q_WIDTH, BF16),
              ("kc", KV_WIDTH, F32), ("vc", KV_WIDTH, F32),
              ("ks", KV_DUP, BF16), ("vs", KV_WIDTH, BF16), ("kw", KV_DUP, BF16), ("vw", KV_WIDTH, BF16),
              ("gn", LANES, F32), ("ga", D_MODEL, BF16), ("gb", D_MODEL, BF16)]
    outs = pl.pallas_call(
        _inproj_kernel,
        grid=(t // tm,),
        in_specs=[row(D_MODEL), fixed(1, D_MODEL), fixed(D_MODEL, _IN_COLS_BIG), row(LANES), row(LANES)],
        out_specs=[row(w) for _, w, _ in out_dt],
        out_shape=[jax.ShapeDtypeStruct((t, w), dt) for _, w, dt in out_dt],
        compiler_params=_params(("parallel",)),
        name="in_proj",
    )(x2, gain.reshape(1, D_MODEL), w_big, cos, sin)
    return {name: o for (name, _, _), o in zip(out_dt, outs)}


def _compress_kernel(flat_ref, pe_ref, w1t_ref, w1b_ref, b1_ref, w2_ref, o_ref):
    f = flat_ref[...]
    pe = pe_ref[...]
    top = jnp.dot((f + pe[0:1]).astype(BF16), w1t_ref[...], preferred_element_type=F32)
    bot = jnp.dot((f + pe[1:2]).astype(BF16), w1b_ref[...], preferred_element_type=F32)
    ncp = f.shape[0]
    hid = _gelu_tanh(top + pltpu.roll(bot, ncp - 1, axis=0) + b1_ref[...])
    o_ref[...] = jnp.dot(hid.astype(BF16), w2_ref[...], preferred_element_type=F32).astype(BF16)


def _compress_weights(pos_emb, w1, b1, w2):
    half = CMP_LEN // 2
    eye = jnp.eye(KV_HEADS, dtype=F32)
    w1r = w1.reshape(2, CMP_LEN, HEAD_DIM, CMP_HIDDEN)

    def big(part):
        return jnp.einsum("wtdh,gk->wtgdkh", part, eye).reshape(
            2, half * KV_WIDTH, KV_HEADS * CMP_HIDDEN).astype(BF16)

    def pe_big(part):
        return jnp.broadcast_to(part[:, :, None, :], (2, half, KV_HEADS, HEAD_DIM)).reshape(2, half * KV_WIDTH)

    pe = jnp.stack([pe_big(pos_emb[:, :half]), pe_big(pos_emb[:, half:])], axis=1)
    b1b = jnp.tile(b1, (1, KV_HEADS)).reshape(2, 1, KV_HEADS * CMP_HIDDEN)
    w2b = jnp.einsum("whd,gk,r->wghkrd", w2, eye, jnp.ones((2,), F32)).reshape(
        2, KV_HEADS * CMP_HIDDEN, KV_DUP).astype(BF16)
    return pe, big(w1r[:, :half]), big(w1r[:, half:]), b1b, w2b


def _compress(flat, cw):
    pe, w1t, w1b, b1b, w2b = cw
    _, b, ncp, fw = flat.shape
    hw = KV_HEADS * CMP_HIDDEN
    per_w = lambda *shape: pl.BlockSpec((None,) + shape, lambda w, bb: (w,) + (0,) * len(shape))
    return pl.pallas_call(
        _compress_kernel,
        grid=(2, b),
        in_specs=[pl.BlockSpec((None, None, ncp, fw), lambda w, bb: (w, bb, 0, 0)),
                  per_w(2, fw), per_w(fw, hw), per_w(fw, hw), per_w(1, hw), per_w(hw, KV_DUP)],
        out_specs=pl.BlockSpec((None, None, ncp, KV_DUP), lambda w, bb: (w, bb, 0, 0)),
        out_shape=jax.ShapeDtypeStruct((2, b, ncp, KV_DUP), BF16),
        compiler_params=_params(("parallel", "parallel")),
        name="compress",
    )(flat, pe, w1t, w1b, b1b, w2b)


def _stack_heads(q):
    lane = lax.broadcasted_iota(jnp.int32, (BLOCK, LANES), 1)
    lo = lane < HEAD_DIM
    zero = jnp.zeros((BLOCK, LANES), q.dtype)
    p0, p1 = q[:, :LANES], q[:, LANES:]
    return jnp.concatenate([jnp.where(lo, p0, zero), jnp.where(lo, zero, p0),
                            jnp.where(lo, p1, zero), jnp.where(lo, zero, p1)], axis=0)


def _unstack_out(o_t):
    lane = lax.broadcasted_iota(jnp.int32, (BLOCK, LANES), 1)
    lo = lane < HEAD_DIM
    o2 = jnp.concatenate([o_t, o_t], axis=0)
    t = [o2[:, h * BLOCK:(h + 1) * BLOCK].T for h in range(GROUP)]
    return jnp.concatenate([jnp.where(lo, t[0], t[1]), jnp.where(lo, t[2], t[3])], axis=1)


def _gate_row(g_ref, branch):
    g = g_ref[...]
    return jnp.concatenate([g[branch * GROUP + h:branch * GROUP + h + 1, :] for h in range(GROUP)], axis=1)


def _nt_dot(a, b):
    return lax.dot_general(a, b, (((1,), (1,)), ((), ())), preferred_element_type=F32)


def _attn_specs(s):
    nb = s // BLOCK
    q_spec = pl.BlockSpec((None, BLOCK, GROUP * HEAD_DIM), lambda b, g, i: (b, i, g))
    k_spec = pl.BlockSpec((None, s, LANES), lambda b, g, i: (b, 0, g))
    vt_spec = pl.BlockSpec((None, HEAD_DIM, s), lambda b, g, i: (b, g, 0))
    gate_spec = pl.BlockSpec((None, None, 16, BLOCK), lambda b, g, i: (b, g, 0, i))
    return nb, q_spec, k_spec, vt_spec, gate_spec


def _band_kernel(*refs, window, nkb, use_sink, gate_branch, has_prev):
    refs = list(refs)
    q_ref, k_ref, vt_ref = refs[:3]
    pos = 3
    g_ref = sink_ref = prev_ref = None
    if gate_branch is not None:
        g_ref = refs[pos]; pos += 1
    if use_sink:
        sink_ref = refs[pos]; pos += 1
    if has_prev:
        prev_ref = refs[pos]; pos += 1
    o_ref = refs[pos]

    i = pl.program_id(2)
    length = nkb * BLOCK
    start = pl.multiple_of(jnp.maximum(i - (nkb - 1), 0) * BLOCK, BLOCK)
    kb = k_ref[pl.ds(start, length), :]
    vb = vt_ref[:, pl.ds(start, length)]
    qs = _stack_heads(q_ref[...])
    kpos = start + lax.broadcasted_iota(jnp.int32, (length, BLOCK), 0)
    qpos = i * BLOCK + lax.broadcasted_iota(jnp.int32, (length, BLOCK), 1)
    rel = qpos - kpos
    bias = jnp.where((rel >= 0) & (rel < window), 0.0, NEG_INF).astype(F32)
    s = _nt_dot(kb, qs) + jnp.concatenate([bias] * GROUP, axis=1)
    m = jnp.max(s, axis=0, keepdims=True)
    if use_sink:
        sink = sink_ref[...]
        m = jnp.maximum(m, sink)
    p = jnp.exp2(s - m)
    l = jnp.sum(p, axis=0, keepdims=True)
    if use_sink:
        l = l + jnp.exp2(sink - m)
    o_t = jnp.dot(vb, p.astype(BF16), preferred_element_type=F32)
    scale = 1.0 / jnp.maximum(l, TINY)
    if gate_branch is not None:
        scale = scale * _gate_row(g_ref, gate_branch)
    out = _unstack_out(o_t * scale)
    if has_prev:
        out = out + prev_ref[...]
    o_ref[...] = out.astype(o_ref.dtype)


def _band_attention(q, k, vt, *, window, gates=None, gate_branch=None, sinks=None, prev=None, out_dtype=F32):
    b, s, _ = q.shape
    nb, q_spec, k_spec, vt_spec, gate_spec = _attn_specs(s)
    nkb = -(-window // BLOCK) + 1
    ins, specs = [q, k, vt], [q_spec, k_spec, vt_spec]
    if gate_branch is not None:
        ins.append(gates); specs.append(gate_spec)
    if sinks is not None:
        ins.append(sinks)
        specs.append(pl.BlockSpec((None, 1, GROUP * BLOCK), lambda b_, g, i: (g, 0, 0)))
    if prev is not None:
        ins.append(prev); specs.append(q_spec)
    kern = functools.partial(_band_kernel, window=window, nkb=nkb, use_sink=sinks is not None,
                             gate_branch=gate_branch, has_prev=prev is not None)
    return pl.pallas_call(
        kern,
        grid=(b, KV_HEADS, nb),
        in_specs=specs,
        out_specs=q_spec,
        out_shape=jax.ShapeDtypeStruct((b, s, Q_WIDTH), out_dtype),
        compiler_params=_params(("parallel", "parallel", "parallel")),
        name=f"band_attn_w{window}",
    )(*ins)


def _cmp_kernel(q_ref, kc_ref, vct_ref, g_ref, mt_ref, o_ref, sel_ref, *, n_sel):
    i = pl.program_id(2)
    qs = _stack_heads(q_ref[...])
    ncp = kc_ref.shape[0]
    c = lax.broadcasted_iota(jnp.int32, (ncp, BLOCK), 0)
    t = i * BLOCK + lax.broadcasted_iota(jnp.int32, (1, BLOCK), 1)
    last = (t - (CMP_LEN - 1)) >> (CMP_STRIDE.bit_length() - 1)
    bias = jnp.where(c <= last, 0.0, NEG_INF).astype(F32)
    s = _nt_dot(kc_ref[...], qs) + jnp.concatenate([bias] * GROUP, axis=1)
    m = jnp.max(s, axis=0, keepdims=True)
    m = jnp.where(m == NEG_INF, 0.0, m)
    p = jnp.exp2(s - m)
    l = jnp.sum(p, axis=0, keepdims=True)
    p = p * (1.0 / jnp.maximum(l, TINY))
    o_t = jnp.dot(vct_ref[...], p.astype(BF16), preferred_element_type=F32)
    o_ref[...] = _unstack_out(o_t * _gate_row(g_ref, 0))

    imp = p[:, 0:BLOCK]
    for h in range(1, GROUP):
        imp = imp + p[:, h * BLOCK:(h + 1) * BLOCK]
    hi = imp.astype(BF16)
    r1 = imp - hi.astype(F32)
    mid = r1.astype(BF16)
    lo = (r1 - mid.astype(F32)).astype(BF16)
    mt = mt_ref[...]
    imp_s = (jnp.dot(mt, hi, preferred_element_type=F32) + jnp.dot(mt, mid, preferred_element_type=F32)
             + jnp.dot(mt, lo, preferred_element_type=F32))

    ns = imp_s.shape[0]
    j = lax.broadcasted_iota(jnp.int32, imp_s.shape, 0)
    tq = i * BLOCK + lax.broadcasted_iota(jnp.int32, imp_s.shape, 1)
    cur = tq >> (SLC_LEN.bit_length() - 1)
    valid = j <= cur
    forced = (j == 0) | (valid & (j > cur - N_LOCAL_SLC))
    work = jnp.where(valid & jnp.logical_not(forced), imp_s, NEG_INF)
    sel_bias = jnp.where(forced, 0.0, NEG_BIG).astype(F32)
    for _ in range(n_sel - 1 - N_LOCAL_SLC):
        mx = jnp.max(work, axis=0, keepdims=True)
        first = jnp.min(jnp.where(work == mx, j, ns), axis=0, keepdims=True)
        pick = j == first
        sel_bias = jnp.where(pick, 0.0, sel_bias)
        work = jnp.where(pick, NEG_INF, work)
    sel_bias = jnp.where(valid, sel_bias, NEG_BIG)
    sel_q = sel_bias.T.astype(BF16)
    pad = sel_ref.shape[1] - ns
    sel_ref[...] = jnp.concatenate([sel_q, jnp.full((BLOCK, pad), NEG_BIG, BF16)], axis=1)


def _overlap_map(ns, ncp):
    ratio_c = CMP_LEN // CMP_STRIDE
    ratio_s = SLC_LEN // CMP_STRIDE
    m = np.zeros((ns, ncp), np.float32)
    for jj in range(ns):
        for a in range(ratio_s):
            for n in range(ratio_c):
                c = ratio_s * jj + a - n
                if 0 <= c < ncp - ratio_c + 1:
                    m[jj, c] += 1.0
    return jnp.asarray(m, BF16)


def _cmp_attention(q_raw, kc, vct, gates):
    b, s, _ = q_raw.shape
    nb, q_spec, _, _, gate_spec = _attn_specs(s)
    ncp = s // CMP_STRIDE
    ns = s // SLC_LEN
    assert ns >= N_SLC
    nsp = -(-ns // LANES) * LANES + LANES
    mt = _overlap_map(ns, ncp)
    kern = functools.partial(_cmp_kernel, n_sel=N_SLC)
    return pl.pallas_call(
        kern,
        grid=(b, KV_HEADS, nb),
        in_specs=[q_spec,
                  pl.BlockSpec((None, ncp, LANES), lambda b_, g, i: (b_, 0, g)),
                  pl.BlockSpec((None, HEAD_DIM, ncp), lambda b_, g, i: (b_, 2 * g, 0)),
                  gate_spec,
                  pl.BlockSpec((ns, ncp), lambda b_, g, i: (0, 0))],
        out_specs=[q_spec, pl.BlockSpec((None, None, BLOCK, nsp), lambda b_, g, i: (b_, g, i, 0))],
        out_shape=[jax.ShapeDtypeStruct((b, s, Q_WIDTH), F32),
                   jax.ShapeDtypeStruct((b, KV_HEADS, s, nsp), BF16)],
        compiler_params=_params(("parallel", "parallel", "parallel")),
        name="cmp_attn_topk",
    )(q_raw, kc, vct, gates, mt)


SEL_BLOCKS_PER_TILE = SEL_TILE // SLC_LEN
SEL_TILES_PER_LANES = LANES // SEL_BLOCKS_PER_TILE
SEL_UNROLL = 4


def _sel_kernel(q_ref, k_ref, vt_ref, sel_ref, oh_ref, g_ref, prev_ref, o_ref,
                sa_sc, sb_sc, mxa_sc, mxb_sc, m_sc, l_sc, acc_sc):
    i = pl.program_id(2)
    tk = SEL_TILE
    qs = _stack_heads(q_ref[...])
    diag = i // (tk // BLOCK)
    masked_lanes = sel_ref.shape[1] - LANES
    bufs = ((sa_sc, mxa_sc), (sb_sc, mxb_sc))

    def tile_of(step):
        below = jnp.clip(step - 1, 0, jnp.maximum(diag - 1, 0))
        return jnp.where(step == 0, diag, below)

    def scores(tile, lane0):
        k0 = pl.multiple_of(tile * tk, tk)
        feat = sel_ref[:, pl.ds(pl.multiple_of(lane0, LANES), LANES)]
        qa = jnp.concatenate([qs, jnp.concatenate([feat] * GROUP, axis=0)], axis=1)
        ka = jnp.concatenate([k_ref[pl.ds(k0, tk), :], oh_ref[tile % SEL_TILES_PER_LANES]], axis=1)
        return _nt_dot(ka, qa)

    def produce(step, buf, s_extra=None):
        tile = tile_of(step)
        lane0 = jnp.where(step <= diag, (tile // SEL_TILES_PER_LANES) * LANES, masked_lanes)
        sn = scores(tile, lane0)
        if s_extra is not None:
            sn = sn + s_extra
        buf[0][...] = sn
        buf[1][...] = jnp.max(sn, axis=0, keepdims=True)

    def consume(step, buf):
        k0 = pl.multiple_of(tile_of(step) * tk, tk)
        m_old = m_sc[...]
        m_new = jnp.maximum(m_old, buf[1][...])
        alpha = jnp.exp2(m_old - m_new)
        p = jnp.exp2(buf[0][...] - m_new)
        l_sc[...] = alpha * l_sc[...] + jnp.sum(p, axis=0, keepdims=True)
        acc_sc[...] = alpha * acc_sc[...] + jnp.dot(vt_ref[:, pl.ds(k0, tk)], p.astype(BF16),
                                                    preferred_element_type=F32)
        m_sc[...] = m_new

    m_sc[...] = jnp.full(m_sc.shape, NEG_BIG, F32)
    l_sc[...] = jnp.zeros(l_sc.shape, F32)
    acc_sc[...] = jnp.zeros(acc_sc.shape, F32)
    kpos = diag * tk + lax.broadcasted_iota(jnp.int32, (tk, BLOCK), 0)
    qpos = i * BLOCK + lax.broadcasted_iota(jnp.int32, (tk, BLOCK), 1)
    causal = jnp.where(kpos <= qpos, 0.0, NEG_BIG).astype(F32)
    produce(0, bufs[0], jnp.concatenate([causal] * GROUP, axis=1))

    def body(it, carry):
        step = SEL_UNROLL * it
        for u in range(SEL_UNROLL):
            produce(step + u + 1, bufs[(u + 1) % 2])
            consume(step + u, bufs[u % 2])
        return carry

    lax.fori_loop(0, (diag + SEL_UNROLL) // SEL_UNROLL, body, 0)
    scale = _gate_row(g_ref, 1) / jnp.maximum(l_sc[...], TINY)
    o_ref[...] = _unstack_out(acc_sc[...] * scale) + prev_ref[...]


def _block_onehots():
    jm = np.arange(SEL_TILES_PER_LANES)[:, None, None]
    r = np.arange(SEL_TILE)[None, :, None]
    c = np.arange(LANES)[None, None, :]
    return jnp.asarray(c == jm * SEL_BLOCKS_PER_TILE + r // SLC_LEN, BF16)


def _sel_attention(q_rot, ks, vst, sel, gates, prev):
    b, s, _ = q_rot.shape
    nb, q_spec, k_spec, vt_spec, gate_spec = _attn_specs(s)
    nsp = sel.shape[-1]
    cols = GROUP * BLOCK
    return pl.pallas_call(
        _sel_kernel,
        grid=(b, KV_HEADS, nb),
        in_specs=[q_spec, k_spec, vt_spec,
                  pl.BlockSpec((None, None, BLOCK, nsp), lambda b_, g, i: (b_, g, i, 0)),
                  pl.BlockSpec((SEL_TILES_PER_LANES, SEL_TILE, LANES), lambda b_, g, i: (0, 0, 0)),
                  gate_spec, q_spec],
        out_specs=q_spec,
        out_shape=jax.ShapeDtypeStruct((b, s, Q_WIDTH), F32),
        scratch_shapes=[pltpu.VMEM((SEL_TILE, cols), F32), pltpu.VMEM((SEL_TILE, cols), F32),
                        pltpu.VMEM((1, cols), F32), pltpu.VMEM((1, cols), F32),
                        pltpu.VMEM((1, cols), F32), pltpu.VMEM((1, cols), F32),
                        pltpu.VMEM((HEAD_DIM, cols), F32)],
        compiler_params=_params(("parallel", "parallel", "parallel")),
        name="sel_attn",
    )(q_rot, ks, vst, sel, _block_onehots(), gates, prev)


def _merge_kernel(oa_ref, ob_ref, ga_ref, gb_ref, x_ref, wa_ref, wb_ref, wo_ref, gain_ref, o_ref):
    ya = jnp.dot(oa_ref[...], wa_ref[...], preferred_element_type=F32)
    yb = jnp.dot(ob_ref[...].astype(BF16), wb_ref[...], preferred_element_type=F32)
    mixed = ga_ref[...].astype(F32) * ya + gb_ref[...].astype(F32) * yb
    z = jnp.dot(mixed.astype(BF16), wo_ref[...], preferred_element_type=F32)
    o_ref[...] = x_ref[...] + _rms(z, gain_ref[...])


def _merge(o_a, o_b, ga, gb, x2, w_a, w_b, w_o, gain):
    t = x2.shape[0]
    tm = 512
    row = lambda w: pl.BlockSpec((tm, w), lambda i: (i, 0))
    fixed = lambda r, c: pl.BlockSpec((r, c), lambda i: (0, 0))
    return pl.pallas_call(
        _merge_kernel,
        grid=(t // tm,),
        in_specs=[row(Q_WIDTH), row(Q_WIDTH), row(D_MODEL), row(D_MODEL), row(D_MODEL),
                  fixed(Q_WIDTH, D_MODEL), fixed(Q_WIDTH, D_MODEL), fixed(D_MODEL, D_MODEL), fixed(1, D_MODEL)],
        out_specs=row(D_MODEL),
        out_shape=jax.ShapeDtypeStruct((t, D_MODEL), F32),
        compiler_params=_params(("parallel",)),
        name="merge_out_proj",
    )(o_a, o_b, ga, gb, x2, w_a.astype(BF16), w_b.astype(BF16), w_o.astype(BF16), gain.reshape(1, D_MODEL))


def _ffn_kernel(xp_ref, x_ref, gpre_ref, wua_ref, wuv_ref, cwa_ref, cwv_ref, cba_ref, cbv_ref, wd_ref,
                gpost_ref, o_ref, h_sc, ua_sc, uv_sc, acc_sc, *, tm, tiles_per_seq):
    i = pl.program_id(0)
    n = pl.program_id(1)

    @pl.when(n == 0)
    def _():
        gpre = gpre_ref[...]
        keep = (i % tiles_per_seq != 0).astype(F32)
        h_sc[0:FFN_HALO, :] = (_rms(xp_ref[...], gpre) * keep).astype(BF16)
        h_sc[FFN_HALO:, :] = _rms(x_ref[...], gpre).astype(BF16)
        acc_sc[...] = jnp.zeros(acc_sc.shape, F32)

    h = h_sc[...]

    def conv(wu_ref, cw_ref, cb_ref, u_sc):
        u_sc[...] = jnp.dot(h, wu_ref[...], preferred_element_type=F32)
        cw = cw_ref[...]
        return (cb_ref[...] + cw[0:1] * u_sc[pl.ds(FFN_HALO - 2, tm), :]
                + cw[1:2] * u_sc[pl.ds(FFN_HALO - 1, tm), :] + cw[2:3] * u_sc[pl.ds(FFN_HALO, tm), :])

    a = conv(wua_ref, cwa_ref, cba_ref, ua_sc)
    v = conv(wuv_ref, cwv_ref, cbv_ref, uv_sc)
    acc_sc[...] += jnp.dot((_gelu_tanh(a) * v).astype(BF16), wd_ref[...], preferred_element_type=F32)

    @pl.when(n == pl.num_programs(1) - 1)
    def _():
        o_ref[...] = x_ref[...] + _rms(acc_sc[...], gpost_ref[...])


def _ffn(x2, seq, gpre, w_up, conv_w, conv_b, w_down, gpost):
    t = x2.shape[0]
    tm = min(1024, seq)
    ck = FFN_CHUNK
    nck = D_FF // ck
    kern = functools.partial(_ffn_kernel, tm=tm, tiles_per_seq=seq // tm)
    halo_blocks = tm // FFN_HALO
    return pl.pallas_call(
        kern,
        grid=(t // tm, nck),
        in_specs=[pl.BlockSpec((FFN_HALO, D_MODEL), lambda i, n: (jnp.maximum(i * halo_blocks - 1, 0), 0)),
                  pl.BlockSpec((tm, D_MODEL), lambda i, n: (i, 0)),
                  pl.BlockSpec((1, D_MODEL), lambda i, n: (0, 0)),
                  pl.BlockSpec((D_MODEL, ck), lambda i, n: (0, n)),
                  pl.BlockSpec((D_MODEL, ck), lambda i, n: (0, nck + n)),
                  pl.BlockSpec((3, ck), lambda i, n: (0, n)),
                  pl.BlockSpec((3, ck), lambda i, n: (0, nck + n)),
                  pl.BlockSpec((1, ck), lambda i, n: (0, n)),
                  pl.BlockSpec((1, ck), lambda i, n: (0, nck + n)),
                  pl.BlockSpec((ck, D_MODEL), lambda i, n: (n, 0)),
                  pl.BlockSpec((1, D_MODEL), lambda i, n: (0, 0))],
        out_specs=pl.BlockSpec((tm, D_MODEL), lambda i, n: (i, 0)),
        out_shape=jax.ShapeDtypeStruct((t, D_MODEL), F32),
        scratch_shapes=[pltpu.VMEM((tm + FFN_HALO, D_MODEL), BF16),
                        pltpu.VMEM((tm + FFN_HALO, ck), F32),
                        pltpu.VMEM((tm + FFN_HALO, ck), F32),
                        pltpu.VMEM((tm, D_MODEL), F32)],
        compiler_params=_params(("parallel", "arbitrary")),
        name="conv_ffn",
    )(x2, x2, gpre.reshape(1, D_MODEL), w_up.astype(BF16), w_up.astype(BF16), conv_w, conv_w,
      conv_b.reshape(1, 2 * D_FF), conv_b.reshape(1, 2 * D_FF), w_down.astype(BF16), gpost.reshape(1, D_MODEL))


def _to_t(v, b, s):
    return jnp.transpose(v.reshape(b, s, KV_WIDTH), (0, 2, 1))


def _layer(x2, b, s, cos, sin, p):
    t = b * s
    u = _inproj(x2, p["attn_pre_gain"], _inproj_weight(p["w_in"]), cos, sin)
    r3 = lambda a: a.reshape(b, s, a.shape[-1])

    gn = u["gn"][:, :N_GATES].reshape(b, s, KV_HEADS, GROUP, 3)
    gates = jnp.transpose(gn, (0, 2, 4, 3, 1)).reshape(b, KV_HEADS, 3 * GROUP, s)
    gates = jnp.pad(gates, ((0, 0), (0, 0), (0, 16 - 3 * GROUP), (0, 0)))
    sinks = jnp.repeat(p["attn_sinks"].astype(F32) * LOG2E, BLOCK).reshape(KV_HEADS, 1, GROUP * BLOCK)

    o_a = _band_attention(r3(u["qa"]), r3(u["ka"]), _to_t(u["va"], b, s), window=A_WINDOW,
                          sinks=sinks, out_dtype=BF16)

    ncp = s // CMP_STRIDE
    flat = jnp.stack([u["kc"], u["vc"]]).reshape(2, b, ncp, CMP_STRIDE * KV_WIDTH)
    cmp = _compress(flat, _compress_weights(p["cmp_pos_emb"], p["cmp_w1"], p["cmp_b1"], p["cmp_w2"]))
    o_c, sel = _cmp_attention(r3(u["qbraw"]), cmp[0], jnp.transpose(cmp[1], (0, 2, 1)), gates)
    o_cs = _sel_attention(r3(u["qbrot"]), r3(u["ks"]), _to_t(u["vs"], b, s), sel, gates, o_c)
    o_b = _band_attention(r3(u["qbrot"]), r3(u["kw"]), _to_t(u["vw"], b, s), window=B_WINDOW,
                          gates=gates, gate_branch=2, prev=o_cs)

    x2 = _merge(o_a.reshape(t, Q_WIDTH), o_b.reshape(t, Q_WIDTH), u["ga"], u["gb"], x2,
                p["w_branch_a"], p["w_branch_b"], p["w_out"], p["attn_post_gain"])
    return _ffn(x2, s, p["ffn_pre_gain"], p["ffn_w_up"], p["ffn_conv_w"], p["ffn_conv_b"], p["ffn_w_down"],
                p["ffn_post_gain"])


def kernel(x, positions, attn_pre_gain, attn_post_gain, ffn_pre_gain, ffn_post_gain, w_in, attn_sinks,
           cmp_pos_emb, cmp_w1, cmp_b1, cmp_w2, w_branch_a, w_branch_b, w_out, ffn_w_up, ffn_conv_w,
           ffn_conv_b, ffn_w_down):
    b, s, d = x.shape
    params = dict(attn_pre_gain=attn_pre_gain, attn_post_gain=attn_post_gain, ffn_pre_gain=ffn_pre_gain,
                  ffn_post_gain=ffn_post_gain, w_in=w_in, attn_sinks=attn_sinks, cmp_pos_emb=cmp_pos_emb,
                  cmp_w1=cmp_w1, cmp_b1=cmp_b1, cmp_w2=cmp_w2, w_branch_a=w_branch_a, w_branch_b=w_branch_b,
                  w_out=w_out, ffn_w_up=ffn_w_up, ffn_conv_w=ffn_conv_w, ffn_conv_b=ffn_conv_b,
                  ffn_w_down=ffn_w_down)
    cos, sin = _rope_tables(positions)
    x2 = x.reshape(b * s, d)
    for layer in range(attn_pre_gain.shape[0]):
        x2 = _layer(x2, b, s, cos, sin, {k: v[layer] for k, v in params.items()})
    return x2.reshape(b, s, d)
```

```python
import functools

import numpy as np
import jax
import jax.numpy as jnp
from jax import lax
from jax.experimental import pallas as pl
from jax.experimental.pallas import tpu as pltpu

F32 = jnp.float32
BF16 = jnp.bfloat16

D_MODEL = 1024
HEAD_DIM = 64
BLOCK = 128
Q_HEADS = 8
KV_HEADS = 2
GROUP = Q_HEADS // KV_HEADS
A_WINDOW = 128
B_WINDOW = 512
CMP_LEN = 32
CMP_STRIDE = 16
CMP_HIDDEN = 128
SLC_LEN = 64
N_SLC = 16
N_LOCAL_SLC = 2
D_FF = 2816
ROPE_THETA = 10000.0
EPS = 1e-6
TINY = 1e-30
NEG_INF = float("-inf")
LOG2E = 1.4426950408889634
Q_SCALE = HEAD_DIM ** -0.5 * LOG2E
NEG_BIG = -1e30

LANES = 128
VMEM_LIMIT = 56 * 1024 * 1024

Q_WIDTH = Q_HEADS * HEAD_DIM
KV_WIDTH = KV_HEADS * HEAD_DIM
KV_DUP = 2 * KV_WIDTH
N_GATES = 3 * Q_HEADS
GATE_ROWS = 16

SEL_TILE = 1024
BAND_QBLOCKS = 8
FFN_CHUNK = 256
FFN_HALO = 16


def _params(sem, flags=None):
    return pltpu.CompilerParams(dimension_semantics=sem, vmem_limit_bytes=VMEM_LIMIT, flags=flags)


def _rms(x, gain):
    return x * lax.rsqrt(jnp.mean(x * x, axis=-1, keepdims=True) + EPS) * gain


def _gelu_tanh(x):
    return 0.5 * x * (1.0 + jnp.tanh(0.7978845608028654 * (x + 0.044715 * (x * x * x))))


def _sigmoid(x):
    return 1.0 / (1.0 + jnp.exp(-x))


def _rope_table_kernel(pos_ref, invf_ref, cos_ref, sin_ref):
    ang = pos_ref[...].astype(F32) * invf_ref[...]
    lane = lax.broadcasted_iota(jnp.int32, ang.shape, 1)
    sign = jnp.where((lane & (HEAD_DIM - 1)) < HEAD_DIM // 2, -1.0, 1.0).astype(F32)
    cos_ref[...] = jnp.cos(ang)
    sin_ref[...] = jnp.sin(ang) * sign


def _rope_tables(positions):
    t = positions.size
    tm = min(1024, t)
    half = HEAD_DIM // 2
    inv_freq = ROPE_THETA ** (-jnp.arange(half, dtype=F32) / half)
    invf = jnp.tile(inv_freq, LANES // half).reshape(1, LANES)
    return pl.pallas_call(
        _rope_table_kernel,
        grid=(t // tm,),
        in_specs=[pl.BlockSpec((tm, 1), lambda i: (i, 0)),
                  pl.BlockSpec((1, LANES), lambda i: (0, 0))],
        out_specs=[pl.BlockSpec((tm, LANES), lambda i: (i, 0))] * 2,
        out_shape=[jax.ShapeDtypeStruct((t, LANES), F32)] * 2,
        compiler_params=_params(("parallel",)),
        name="rope_tables",
    )(positions.reshape(t, 1), invf)


def _rope(u, cos, sin):
    lane = lax.broadcasted_iota(jnp.int32, cos.shape, 1)
    first_half = (lane & (HEAD_DIM - 1)) < HEAD_DIM // 2
    outs = []
    for c in range(u.shape[1] // LANES):
        x = u[:, c * LANES:(c + 1) * LANES]
        partner = jnp.where(first_half,
                            pltpu.roll(x, LANES - HEAD_DIM // 2, axis=1),
                            pltpu.roll(x, HEAD_DIM // 2, axis=1))
        outs.append(x * cos + partner * sin)
    return outs[0] if len(outs) == 1 else jnp.concatenate(outs, axis=1)


_IN_SEGS = (("qa", Q_WIDTH), ("ka", KV_DUP), ("va", KV_WIDTH), ("qb", Q_WIDTH),
            ("kc", KV_WIDTH), ("vc", KV_WIDTH), ("ks", KV_DUP), ("vs", KV_WIDTH),
            ("kw", KV_DUP), ("vw", KV_WIDTH), ("gn", LANES), ("ga", D_MODEL), ("gb", D_MODEL))
_IN_COLS_BIG = sum(w for _, w in _IN_SEGS)


def _inproj_kernel(x_ref, gain_ref, w_ref, cos_ref, sin_ref,
                   qa_ref, ka_ref, vat_ref, qbraw_ref, qbrot_ref, kvc_ref,
                   ks_ref, vst_ref, kw_ref, vwt_ref, gnt_ref, ga_ref, gb_ref):
    h = _rms(x_ref[...], gain_ref[...]).astype(BF16)
    cos = cos_ref[...]
    sin = sin_ref[...]
    offs = {}
    off = 0
    for name, width in _IN_SEGS:
        offs[name] = (off, width)
        off += width

    def proj(name):
        o, w = offs[name]
        return jnp.dot(h, w_ref[:, o:o + w], preferred_element_type=F32)

    qa_ref[...] = (_rope(proj("qa"), cos, sin) * Q_SCALE).astype(BF16)
    ka_ref[...] = _rope(proj("ka"), cos, sin).astype(BF16)
    vat_ref[...] = proj("va").T.astype(BF16)
    qb = proj("qb")
    qbraw_ref[...] = (qb * Q_SCALE).astype(BF16)
    qbrot_ref[...] = (_rope(qb, cos, sin) * Q_SCALE).astype(BF16)
    kvc_ref[0] = proj("kc")
    kvc_ref[1] = proj("vc")
    ks_ref[...] = _rope(proj("ks"), cos, sin).astype(BF16)
    vs_t = proj("vs").T
    tm = vs_t.shape[1]
    ones_row = (lax.broadcasted_iota(jnp.int32, (SEL_V_ROWS - HEAD_DIM, tm), 0) == 0).astype(F32)
    vst_ref[...] = jnp.concatenate(
        [piece for g in range(KV_HEADS) for piece in (vs_t[g * HEAD_DIM:(g + 1) * HEAD_DIM], ones_row)],
        axis=0).astype(BF16)
    kw_ref[...] = _rope(proj("kw"), cos, sin).astype(BF16)
    vwt_ref[...] = proj("vw").T.astype(BF16)
    gnt_ref[...] = _sigmoid(proj("gn")).T[0:KV_HEADS * GATE_ROWS, :]
    ga_ref[...] = _sigmoid(proj("ga")).astype(BF16)
    gb_ref[...] = _sigmoid(proj("gb")).astype(BF16)


def _dup_heads(w):
    g0, g1 = w[:, :HEAD_DIM], w[:, HEAD_DIM:]
    return jnp.concatenate([g0, g0, g1, g1], axis=1)


def _inproj_weight(w_in):
    sizes = (Q_WIDTH, KV_WIDTH, KV_WIDTH, Q_WIDTH) + (KV_WIDTH,) * 6 + (N_GATES, D_MODEL, D_MODEL)
    offs = np.cumsum(sizes)[:-1].tolist()
    qa, ka, va, qb, kc, vc, ks, vs, kw, vw, gn, ga, gb = jnp.split(w_in, offs, axis=1)
    order = [(g * GROUP + h) * 3 + c for g in range(KV_HEADS) for c in range(3) for h in range(GROUP)]
    gn = gn[:, np.asarray(order)].reshape(D_MODEL, KV_HEADS, 3 * GROUP)
    gn = jnp.pad(gn, ((0, 0), (0, 0), (0, GATE_ROWS - 3 * GROUP))).reshape(D_MODEL, KV_HEADS * GATE_ROWS)
    gn = jnp.pad(gn, ((0, 0), (0, LANES - KV_HEADS * GATE_ROWS)))
    cols = [qa, _dup_heads(ka), va, qb, kc, vc, _dup_heads(ks), vs, _dup_heads(kw), vw, gn, ga, gb]
    return jnp.concatenate(cols, axis=1).astype(BF16)


def _inproj(x2, b, s, gain, w_big, cos, sin):
    t = x2.shape[0]
    tm = 256
    spb = s // tm
    row = lambda w: pl.BlockSpec((tm, w), lambda i: (i, 0))
    fixed = lambda r, c: pl.BlockSpec((r, c), lambda i: (0, 0))
    seq_on_lanes = lambda r: pl.BlockSpec((None, r, tm), lambda i: (i // spb, 0, i % spb))
    outs = [("qa", row(Q_WIDTH), (t, Q_WIDTH), BF16),
            ("ka", row(KV_DUP), (t, KV_DUP), BF16),
            ("vat", seq_on_lanes(KV_WIDTH), (b, KV_WIDTH, s), BF16),
            ("qbraw", row(Q_WIDTH), (t, Q_WIDTH), BF16),
            ("qbrot", row(Q_WIDTH), (t, Q_WIDTH), BF16),
            ("kvc", pl.BlockSpec((2, tm, KV_WIDTH), lambda i: (0, i, 0)), (2, t, KV_WIDTH), F32),
            ("ks", row(KV_DUP), (t, KV_DUP), BF16),
            ("vst", seq_on_lanes(KV_HEADS * SEL_V_ROWS), (b, KV_HEADS * SEL_V_ROWS, s), BF16),
            ("kw", row(KV_DUP), (t, KV_DUP), BF16),
            ("vwt", seq_on_lanes(KV_WIDTH), (b, KV_WIDTH, s), BF16),
            ("gnt", seq_on_lanes(KV_HEADS * GATE_ROWS), (b, KV_HEADS * GATE_ROWS, s), F32),
            ("ga", row(D_MODEL), (t, D_MODEL), BF16),
            ("gb", row(D_MODEL), (t, D_MODEL), BF16)]
    res = pl.pallas_call(
        _inproj_kernel,
        grid=(t // tm,),
        in_specs=[row(D_MODEL), fixed(1, D_MODEL), fixed(D_MODEL, _IN_COLS_BIG), row(LANES), row(LANES)],
        out_specs=[spec for _, spec, _, _ in outs],
        out_shape=[jax.ShapeDtypeStruct(shape, dt) for _, _, shape, dt in outs],
        compiler_params=_params(("parallel",)),
        name="in_proj",
    )(x2, gain.reshape(1, D_MODEL), w_big, cos, sin)
    return {name: o for (name, _, _, _), o in zip(outs, res)}


def _compress_kernel(flat_ref, pe_ref, w1t_ref, w1b_ref, b1_ref, w2_ref, o_ref):
    f = flat_ref[...]
    pe = pe_ref[...]
    top = jnp.dot((f + pe[0:1]).astype(BF16), w1t_ref[...], preferred_element_type=F32)
    bot = jnp.dot((f + pe[1:2]).astype(BF16), w1b_ref[...], preferred_element_type=F32)
    ncp = f.shape[0]
    hid = _gelu_tanh(top + pltpu.roll(bot, ncp - 1, axis=0) + b1_ref[...])
    o_ref[...] = jnp.dot(hid.astype(BF16), w2_ref[...], preferred_element_type=F32).astype(BF16)


def _compress_weights(pos_emb, w1, b1, w2):
    half = CMP_LEN // 2
    eye = jnp.eye(KV_HEADS, dtype=F32)
    w1r = w1.reshape(2, CMP_LEN, HEAD_DIM, CMP_HIDDEN)

    def big(part):
        return jnp.einsum("wtdh,gk->wtgdkh", part, eye).reshape(
            2, half * KV_WIDTH, KV_HEADS * CMP_HIDDEN).astype(BF16)

    def pe_big(part):
        return jnp.broadcast_to(part[:, :, None, :], (2, half, KV_HEADS, HEAD_DIM)).reshape(2, half * KV_WIDTH)

    pe = jnp.stack([pe_big(pos_emb[:, :half]), pe_big(pos_emb[:, half:])], axis=1)
    b1b = jnp.tile(b1, (1, KV_HEADS)).reshape(2, 1, KV_HEADS * CMP_HIDDEN)
    w2b = jnp.einsum("whd,gk,r->wghkrd", w2, eye, jnp.ones((2,), F32)).reshape(
        2, KV_HEADS * CMP_HIDDEN, KV_DUP).astype(BF16)
    return pe, big(w1r[:, :half]), big(w1r[:, half:]), b1b, w2b


def _compress(flat, cw):
    pe, w1t, w1b, b1b, w2b = cw
    _, b, ncp, fw = flat.shape
    hw = KV_HEADS * CMP_HIDDEN
    per_w = lambda *shape: pl.BlockSpec((None,) + shape, lambda w, bb: (w,) + (0,) * len(shape))
    return pl.pallas_call(
        _compress_kernel,
        grid=(2, b),
        in_specs=[pl.BlockSpec((None, None, ncp, fw), lambda w, bb: (w, bb, 0, 0)),
                  per_w(2, fw), per_w(fw, hw), per_w(fw, hw), per_w(1, hw), per_w(hw, KV_DUP)],
        out_specs=pl.BlockSpec((None, None, ncp, KV_DUP), lambda w, bb: (w, bb, 0, 0)),
        out_shape=jax.ShapeDtypeStruct((2, b, ncp, KV_DUP), BF16),
        compiler_params=_params(("parallel", "parallel")),
        name="compress",
    )(flat, pe, w1t, w1b, b1b, w2b)


def _stack_heads(q):
    lane = lax.broadcasted_iota(jnp.int32, (BLOCK, LANES), 1)
    lo = lane < HEAD_DIM
    zero = jnp.zeros((BLOCK, LANES), q.dtype)
    p0, p1 = q[:, :LANES], q[:, LANES:]
    return jnp.concatenate([jnp.where(lo, p0, zero), jnp.where(lo, zero, p0),
                            jnp.where(lo, p1, zero), jnp.where(lo, zero, p1)], axis=0)


def _unstack_out(o_t):
    lane = lax.broadcasted_iota(jnp.int32, (BLOCK, LANES), 1)
    lo = lane < HEAD_DIM
    o2 = jnp.concatenate([o_t, o_t], axis=0)
    t = [o2[:, h * BLOCK:(h + 1) * BLOCK].T for h in range(GROUP)]
    return jnp.concatenate([jnp.where(lo, t[0], t[1]), jnp.where(lo, t[2], t[3])], axis=1)


def _gate_row(g, branch):
    return jnp.concatenate([g[branch * GROUP + h:branch * GROUP + h + 1, :] for h in range(GROUP)], axis=1)


def _nt_dot(a, b):
    return lax.dot_general(a, b, (((1,), (1,)), ((), ())), preferred_element_type=F32)


def _attn_specs(s):
    nb = s // BLOCK
    q_spec = pl.BlockSpec((None, BLOCK, GROUP * HEAD_DIM), lambda b, g, i: (b, i, g))
    k_spec = pl.BlockSpec((None, s, LANES), lambda b, g, i: (b, 0, g))
    vt_spec = pl.BlockSpec((None, HEAD_DIM, s), lambda b, g, i: (b, g, 0))
    gate_spec = pl.BlockSpec((None, None, GATE_ROWS,BLOCK), lambda b, g, i: (b, g, 0, i))
    return nb, q_spec, k_spec, vt_spec, gate_spec


def _band_kernel(*refs, window, nkb, use_sink, gate_branch, has_prev):
    refs = list(refs)
    q_ref, k_ref, vt_ref = refs[:3]
    pos = 3
    g_ref = sink_ref = prev_ref = None
    if gate_branch is not None:
        g_ref = refs[pos]; pos += 1
    if use_sink:
        sink_ref = refs[pos]; pos += 1
    if has_prev:
        prev_ref = refs[pos]; pos += 1
    o_ref = refs[pos]

    length = nkb * BLOCK
    for sub in range(BAND_QBLOCKS):
        i = pl.program_id(2) * BAND_QBLOCKS + sub
        rows = slice(sub * BLOCK, (sub + 1) * BLOCK)
        start = pl.multiple_of(jnp.maximum(i - (nkb - 1), 0) * BLOCK, BLOCK)
        kb = k_ref[pl.ds(start, length), :]
        vb = vt_ref[:, pl.ds(start, length)]
        qs = _stack_heads(q_ref[rows, :])
        kpos = start + lax.broadcasted_iota(jnp.int32, (length, BLOCK), 0)
        qpos = i * BLOCK + lax.broadcasted_iota(jnp.int32, (length, BLOCK), 1)
        rel = qpos - kpos
        bias = jnp.where((rel >= 0) & (rel < window), 0.0, NEG_INF).astype(F32)
        s = _nt_dot(kb, qs) + jnp.concatenate([bias] * GROUP, axis=1)
        m = jnp.max(s, axis=0, keepdims=True)
        if use_sink:
            sink = sink_ref[...]
            m = jnp.maximum(m, sink)
        p = jnp.exp2(s - m)
        l = jnp.sum(p, axis=0, keepdims=True)
        if use_sink:
            l = l + jnp.exp2(sink - m)
        o_t = jnp.dot(vb, p.astype(BF16), preferred_element_type=F32)
        scale = 1.0 / jnp.maximum(l, TINY)
        if gate_branch is not None:
            scale = scale * _gate_row(g_ref[:, rows], gate_branch)
        out = _unstack_out(o_t * scale)
        if has_prev:
            out = out + prev_ref[rows, :]
        o_ref[rows, :] = out.astype(o_ref.dtype)


def _band_attention(q, k, vt, *, window, gates=None, gate_branch=None, sinks=None, prev=None, out_dtype=F32):
    b, s, _ = q.shape
    nb, _, k_spec, vt_spec, _ = _attn_specs(s)
    rows = BAND_QBLOCKS * BLOCK
    q_spec = pl.BlockSpec((None, rows, GROUP * HEAD_DIM), lambda b_, g, i: (b_, i, g))
    gate_spec = pl.BlockSpec((None, None, GATE_ROWS,rows), lambda b_, g, i: (b_, g, 0, i))
    nkb = -(-window // BLOCK) + 1
    ins, specs = [q, k, vt], [q_spec, k_spec, vt_spec]
    if gate_branch is not None:
        ins.append(gates); specs.append(gate_spec)
    if sinks is not None:
        ins.append(sinks)
        specs.append(pl.BlockSpec((None, 1, GROUP * BLOCK), lambda b_, g, i: (g, 0, 0)))
    if prev is not None:
        ins.append(prev); specs.append(q_spec)
    kern = functools.partial(_band_kernel, window=window, nkb=nkb, use_sink=sinks is not None,
                             gate_branch=gate_branch, has_prev=prev is not None)
    return pl.pallas_call(
        kern,
        grid=(b, KV_HEADS, nb // BAND_QBLOCKS),
        in_specs=specs,
        out_specs=q_spec,
        out_shape=jax.ShapeDtypeStruct((b, s, Q_WIDTH), out_dtype),
        compiler_params=_params(("parallel", "parallel", "parallel")),
        name=f"band_attn_w{window}",
    )(*ins)


def _cmp_kernel(q_ref, kc_ref, vct_ref, g_ref, mt_ref, o_ref, sel_ref, *, n_sel):
    step = pl.program_id(2)
    steps = pl.num_programs(2)
    ncp_all, ns_all = kc_ref.shape[0], mt_ref.shape[0]

    def prefix(ncp, ns):
        for sub in range(CMP_QBLOCKS):
            _cmp_block(step * CMP_QBLOCKS + sub, slice(sub * BLOCK, (sub + 1) * BLOCK),
                       q_ref, kc_ref, vct_ref, g_ref, mt_ref, o_ref, sel_ref, n_sel=n_sel, ncp=ncp, ns=ns)

    for k in range(CMP_PREFIXES):
        ncp = ncp_all * (k + 1) // CMP_PREFIXES
        ns = min(ns_all, -(-(ns_all * (k + 1) // CMP_PREFIXES) // LANES) * LANES)
        pl.when(step * CMP_PREFIXES // steps == k)(functools.partial(prefix, ncp, ns))


def _cmp_block(i, rows, q_ref, kc_ref, vct_ref, g_ref, mt_ref, o_ref, sel_ref, *, n_sel, ncp, ns):
    qs = _stack_heads(q_ref[rows, :])
    c = lax.broadcasted_iota(jnp.int32, (ncp, BLOCK), 0)
    t = i * BLOCK + lax.broadcasted_iota(jnp.int32, (1, BLOCK), 1)
    last = (t - (CMP_LEN - 1)) >> (CMP_STRIDE.bit_length() - 1)
    bias = jnp.where(c <= last, 0.0, NEG_INF).astype(F32)
    s = _nt_dot(kc_ref[0:ncp, :], qs) + jnp.concatenate([bias] * GROUP, axis=1)
    m = jnp.max(s, axis=0, keepdims=True)
    m = jnp.where(m == NEG_INF, 0.0, m)
    p = jnp.exp2(s - m)
    l = jnp.sum(p, axis=0, keepdims=True)
    p = p * (1.0 / jnp.maximum(l, TINY))
    o_t = jnp.dot(vct_ref[:, 0:ncp], p.astype(BF16), preferred_element_type=F32)
    o_ref[rows, :] = _unstack_out(o_t * _gate_row(g_ref[:, rows], 0))

    imp = p[:, 0:BLOCK]
    for h in range(1, GROUP):
        imp = imp + p[:, h * BLOCK:(h + 1) * BLOCK]
    hi = imp.astype(BF16)
    r1 = imp - hi.astype(F32)
    mid = r1.astype(BF16)
    lo = (r1 - mid.astype(F32)).astype(BF16)
    mt = mt_ref[0:ns, 0:ncp]
    imp_s = (jnp.dot(mt, hi, preferred_element_type=F32) + jnp.dot(mt, mid, preferred_element_type=F32)
             + jnp.dot(mt, lo, preferred_element_type=F32))

    j =lax.broadcasted_iota(jnp.int32, imp_s.shape, 0)
    tq = i * BLOCK + lax.broadcasted_iota(jnp.int32, imp_s.shape, 1)
    cur = tq >> (SLC_LEN.bit_length() - 1)
    valid = j <= cur
    forced = (j == 0) | (valid & (j > cur - N_LOCAL_SLC))
    work = jnp.where(valid & jnp.logical_not(forced), imp_s, NEG_INF)
    sel_bias = jnp.where(forced, 0.0, NEG_BIG).astype(F32)
    for _ in range(n_sel - 1 - N_LOCAL_SLC):
        mx = jnp.max(work, axis=0, keepdims=True)
        first = jnp.min(jnp.where(work == mx, j, ns), axis=0, keepdims=True)
        pick = j == first
        sel_bias = jnp.where(pick, 0.0, sel_bias)
        work = jnp.where(pick, NEG_INF, work)
    sel_bias = jnp.where(valid, sel_bias, NEG_BIG)
    sel_q = sel_bias.T.astype(BF16)
    pad = sel_ref.shape[1] - ns
    sel_ref[rows, :] = jnp.concatenate([sel_q, jnp.full((BLOCK, pad), NEG_BIG, BF16)], axis=1)


def _overlap_map(ns, ncp):
    ratio_c = CMP_LEN // CMP_STRIDE
    ratio_s = SLC_LEN // CMP_STRIDE
    m = np.zeros((ns, ncp), np.float32)
    for jj in range(ns):
        for a in range(ratio_s):
            for n in range(ratio_c):
                c = ratio_s * jj + a - n
                if 0 <= c < ncp - ratio_c + 1:
                    m[jj, c] += 1.0
    return jnp.asarray(m, BF16)


def _cmp_attention(q_raw, kc, vct, gates):
    b, s, _ = q_raw.shape
    nb = s // BLOCK
    rows = CMP_QBLOCKS * BLOCK
    assert nb % (CMP_QBLOCKS * CMP_PREFIXES) == 0
    q_spec = pl.BlockSpec((None, rows, GROUP * HEAD_DIM), lambda b_, g, i: (b_, i, g))
    gate_spec = pl.BlockSpec((None, None, GATE_ROWS,rows), lambda b_, g, i: (b_, g, 0, i))
    ncp = s // CMP_STRIDE
    ns = s // SLC_LEN
    assert ns >= N_SLC
    nsp = -(-ns // LANES) * LANES + LANES
    mt = _overlap_map(ns, ncp)
    kern = functools.partial(_cmp_kernel, n_sel=N_SLC)
    return pl.pallas_call(
        kern,
        grid=(b, KV_HEADS, nb // CMP_QBLOCKS),
        in_specs=[q_spec,
                  pl.BlockSpec((None, ncp, LANES), lambda b_, g, i: (b_, 0, g)),
                  pl.BlockSpec((None, HEAD_DIM, ncp), lambda b_, g, i: (b_, 2 * g, 0)),
                  gate_spec,
                  pl.BlockSpec((ns, ncp), lambda b_, g, i: (0, 0))],
        out_specs=[q_spec, pl.BlockSpec((None, None, rows, nsp), lambda b_, g, i: (b_, g, i, 0))],
        out_shape=[jax.ShapeDtypeStruct((b, s, Q_WIDTH), F32),
                   jax.ShapeDtypeStruct((b, KV_HEADS, s, nsp), BF16)],
        compiler_params=_params(("parallel", "parallel", "parallel")),
        name="cmp_attn_topk",
    )(q_raw, kc, vct, gates, mt)


CMP_PREFIXES = 4
CMP_QBLOCKS = 2
SEL_BLOCKS_PER_TILE = SEL_TILE // SLC_LEN
SEL_TILES_PER_LANES = LANES // SEL_BLOCKS_PER_TILE
SEL_DIAG_VARIANTS = 4
SEL_V_ROWS = HEAD_DIM + 16


def _sel_kernel(q_ref, k_ref, vt_ref, sel_ref, oh_ref, g_ref, prev_ref, o_ref,
                s0_sc, sa_sc, sb_sc, mxa_sc, mxb_sc, ra_sc, rb_sc, m_sc, acc_sc):
    i = pl.program_id(2)
    tk = SEL_TILE
    qs = _stack_heads(q_ref[...])
    diag = i // (tk // BLOCK)
    masked_lanes = sel_ref.shape[1] - LANES
    bufs = ((sa_sc, mxa_sc, ra_sc), (sb_sc, mxb_sc, rb_sc))

    def scores(tile, lane0):
        k0 = pl.multiple_of(tile * tk, tk)
        feat = sel_ref[:, pl.ds(pl.multiple_of(lane0, LANES), LANES)]
        qa = jnp.concatenate([qs, jnp.concatenate([feat] * GROUP, axis=0)], axis=1)
        ka = jnp.concatenate([k_ref[pl.ds(k0, tk), :], oh_ref[tile % SEL_TILES_PER_LANES]], axis=1)
        return _nt_dot(ka, qa)

    def values(tile):
        return vt_ref[:, pl.ds(pl.multiple_of(tile * tk, tk), tk)]

    def tile_or_pad(tile):
        return jnp.minimum(tile, jnp.maximum(diag - 1, 0))

    def produce(tile, buf):
        tc = tile_or_pad(tile)
        lane0 = jnp.where(tile < diag, (tc // SEL_TILES_PER_LANES) * LANES, masked_lanes)
        sn = scores(tc, lane0)
        ref = m_sc[...]
        buf[0][...] = (sn - ref).astype(BF16)
        buf[1][...] = jnp.max(sn, axis=0, keepdims=True)
        buf[2][...] = ref

    def consume(tile, buf):
        m_old = m_sc[...]
        m_new = jnp.maximum(m_old, buf[1][...])
        alpha = jnp.exp2(m_old - m_new)
        shift = (buf[2][...] - m_new).astype(BF16)
        p = jnp.exp2(buf[0][...] + shift)
        acc_sc[...] = alpha * acc_sc[...] + jnp.dot(values(tile_or_pad(tile)), p, preferred_element_type=F32)
        m_sc[...] = m_new

    def diagonal(rows):
        k0 = pl.multiple_of(diag * tk, tk)
        lane0 = pl.multiple_of((diag // SEL_TILES_PER_LANES) * LANES, LANES)
        qa = jnp.concatenate([qs, jnp.concatenate([sel_ref[:, pl.ds(lane0, LANES)]] * GROUP, axis=0)], axis=1)
        ka = jnp.concatenate([k_ref[pl.ds(k0, rows), :], oh_ref[diag % SEL_TILES_PER_LANES, 0:rows, :]], axis=1)
        kpos = k0 + lax.broadcasted_iota(jnp.int32, (rows, BLOCK), 0)
        qpos = i * BLOCK + lax.broadcasted_iota(jnp.int32, (rows, BLOCK), 1)
        causal = jnp.where(kpos <= qpos, 0.0, NEG_BIG).astype(F32)
        s0 = _nt_dot(ka, qa) + jnp.concatenate([causal] * GROUP, axis=1)
        m0 = jnp.max(s0, axis=0, keepdims=True)
        m_sc[...] = m0
        s0_sc[0:rows, :] = s0
        produce(0, bufs[0])
        p0 = jnp.exp2(s0_sc[0:rows, :] - m0).astype(BF16)
        acc_sc[...] = jnp.dot(vt_ref[:, pl.ds(k0, rows)], p0, preferred_element_type=F32)

    blocks_per_tile = tk // BLOCK
    for v in range(SEL_DIAG_VARIANTS):
        rows = tk * (v + 1) // SEL_DIAG_VARIANTS
        pl.when((i % blocks_per_tile) * SEL_DIAG_VARIANTS // blocks_per_tile == v)(
            functools.partial(diagonal, rows))

    def body(pair, carry):
        tile = 2 * pair
        produce(tile + 1, bufs[1])
        consume(tile, bufs[0])
        produce(tile + 2, bufs[0])
        consume(tile + 1, bufs[1])
        return carry

    lax.fori_loop(0, (diag + 1) // 2, body, 0)
    acc = acc_sc[...]
    scale = _gate_row(g_ref[...], 1) / jnp.maximum(acc[HEAD_DIM:HEAD_DIM + 1, :], TINY)
    o_ref[...] = _unstack_out(acc[0:HEAD_DIM, :] * scale) + prev_ref[...]


def _block_onehots():
    jm = np.arange(SEL_TILES_PER_LANES)[:, None, None]
    r = np.arange(SEL_TILE)[None, :, None]
    c = np.arange(LANES)[None, None, :]
    return jnp.asarray(c == jm * SEL_BLOCKS_PER_TILE + r // SLC_LEN, BF16)


def _sel_attention(q_rot, ks, vst, sel, gates, prev):
    b, s, _ = q_rot.shape
    nb, q_spec, k_spec, _, gate_spec = _attn_specs(s)
    vt_spec = pl.BlockSpec((None, SEL_V_ROWS, s), lambda b_, g, i: (b_, g, 0))
    nsp = sel.shape[-1]
    cols = GROUP * BLOCK
    return pl.pallas_call(
        _sel_kernel,
        grid=(b, KV_HEADS, nb),
        in_specs=[q_spec, k_spec, vt_spec,
                  pl.BlockSpec((None, None, BLOCK, nsp), lambda b_, g, i: (b_, g, i, 0)),
                  pl.BlockSpec((SEL_TILES_PER_LANES, SEL_TILE, LANES), lambda b_, g, i: (0, 0, 0)),
                  gate_spec, q_spec],
        out_specs=q_spec,
        out_shape=jax.ShapeDtypeStruct((b, s, Q_WIDTH), F32),
        scratch_shapes=[pltpu.VMEM((SEL_TILE, cols), F32),
                        pltpu.VMEM((SEL_TILE, cols), BF16), pltpu.VMEM((SEL_TILE, cols), BF16),
                        pltpu.VMEM((1, cols), F32), pltpu.VMEM((1, cols), F32),
                        pltpu.VMEM((1, cols), F32), pltpu.VMEM((1, cols), F32),
                        pltpu.VMEM((1, cols), F32),
                        pltpu.VMEM((SEL_V_ROWS, cols), F32)],
        compiler_params=_params(("parallel", "parallel", "parallel")),
        name="sel_attn",
    )(q_rot, ks, vst, sel, _block_onehots(), gates, prev)


def _merge_kernel(oa_ref, ob_ref, ga_ref, gb_ref, x_ref, wa_ref, wb_ref, wo_ref, gain_ref, o_ref):
    ya = jnp.dot(oa_ref[...], wa_ref[...], preferred_element_type=F32)
    yb = jnp.dot(ob_ref[...].astype(BF16), wb_ref[...], preferred_element_type=F32)
    mixed = ga_ref[...].astype(F32) * ya + gb_ref[...].astype(F32) * yb
    z = jnp.dot(mixed.astype(BF16), wo_ref[...], preferred_element_type=F32)
    o_ref[...] = x_ref[...] + _rms(z, gain_ref[...])


def _merge(o_a, o_b, ga, gb, x2, w_a, w_b, w_o, gain):
    t = x2.shape[0]
    tm = 512
    row = lambda w: pl.BlockSpec((tm, w), lambda i: (i, 0))
    fixed = lambda r, c: pl.BlockSpec((r, c), lambda i: (0, 0))
    return pl.pallas_call(
        _merge_kernel,
        grid=(t // tm,),
        in_specs=[row(Q_WIDTH), row(Q_WIDTH), row(D_MODEL), row(D_MODEL), row(D_MODEL),
                  fixed(Q_WIDTH, D_MODEL), fixed(Q_WIDTH, D_MODEL), fixed(D_MODEL, D_MODEL), fixed(1, D_MODEL)],
        out_specs=row(D_MODEL),
        out_shape=jax.ShapeDtypeStruct((t, D_MODEL), F32),
        compiler_params=_params(("parallel",)),
        name="merge_out_proj",
    )(o_a, o_b, ga, gb, x2, w_a.astype(BF16), w_b.astype(BF16), w_o.astype(BF16), gain.reshape(1, D_MODEL))


def _ffn_kernel(xp_ref, x_ref, gpre_ref, wu_ref, cw_ref, cb_ref, wd_ref, gpost_ref, o_ref,
                h_sc, ua_sc, ub_sc, g_sc, *, tm, tiles_per_seq, nck):
    i = pl.program_id(0)
    gpre = gpre_ref[...]
    keep = (i % tiles_per_seq != 0).astype(F32)
    h_sc[0:FFN_HALO, :] = (_rms(xp_ref[...], gpre) * keep).astype(BF16)
    h_sc[FFN_HALO:, :] = _rms(x_ref[...], gpre).astype(BF16)

    def produce(n, u_sc):
        h = h_sc[...]
        u_sc[0] = jnp.dot(h, wu_ref[n], preferred_element_type=F32)
        u_sc[1] = jnp.dot(h, wu_ref[nck + n], preferred_element_type=F32)

    def consume(n, u_sc):
        def conv(part, col):
            cw = cw_ref[col]
            return (cb_ref[col] + cw[0:1] * u_sc[part, pl.ds(FFN_HALO - 2, tm), :]
                    + cw[1:2] * u_sc[part, pl.ds(FFN_HALO - 1, tm), :]
                    + cw[2:3] * u_sc[part, pl.ds(FFN_HALO, tm), :])

        g_sc[n] = (_gelu_tanh(conv(0, n)) * conv(1, nck + n)).astype(BF16)

    produce(0, ua_sc)

    def body(pair, carry):
        n = 2 * pair
        produce(n + 1, ub_sc)
        consume(n, ua_sc)
        produce(n + 2, ua_sc)
        consume(n + 1, ub_sc)
        return carry

    lax.fori_loop(0, (nck - 1) // 2, body, 0)
    consume(nck - 1, ua_sc)
    g = jnp.concatenate([g_sc[n] for n in range(nck)], axis=1)
    z = jnp.dot(g, wd_ref[...], preferred_element_type=F32)
    o_ref[...] = x_ref[...] + _rms(z, gpost_ref[...])


def _ffn(x2, seq, gpre, w_up, conv_w, conv_b, w_down, gpost):
    t = x2.shape[0]
    tm = min(512, seq)
    ck = FFN_CHUNK
    nck = D_FF // ck
    assert nck % 2 == 1
    kern = functools.partial(_ffn_kernel, tm=tm, tiles_per_seq=seq // tm, nck=nck)
    halo_blocks = tm // FFN_HALO
    chunks = lambda a: jnp.transpose(a.reshape(a.shape[0], 2 * nck, ck), (1, 0, 2))
    whole = lambda *shape: pl.BlockSpec(shape, lambda i: (0,) * len(shape), pipeline_mode=pl.Buffered(1))
    return pl.pallas_call(
        kern,
        grid=(t // tm,),
        in_specs=[pl.BlockSpec((FFN_HALO, D_MODEL), lambda i: (jnp.maximum(i * halo_blocks - 1, 0), 0)),
                  pl.BlockSpec((tm, D_MODEL), lambda i: (i, 0)),
                  whole(1, D_MODEL),
                  whole(2 * nck, D_MODEL, ck),
                  whole(2 * nck, 3, ck),
                  whole(2 * nck, 1, ck),
                  whole(D_FF, D_MODEL),
                  whole(1, D_MODEL)],
        out_specs=pl.BlockSpec((tm, D_MODEL), lambda i: (i, 0)),
        out_shape=jax.ShapeDtypeStruct((t, D_MODEL), F32),
        scratch_shapes=[pltpu.VMEM((tm + FFN_HALO, D_MODEL), BF16),
                        pltpu.VMEM((2, tm + FFN_HALO, ck), F32),
                        pltpu.VMEM((2, tm + FFN_HALO, ck), F32),
                        pltpu.VMEM((nck, tm, ck), BF16)],
        compiler_params=_params(("parallel",)),
        name="conv_ffn",
    )(x2, x2, gpre.reshape(1, D_MODEL), chunks(w_up.astype(BF16)), chunks(conv_w),
      chunks(conv_b.reshape(1, 2 * D_FF)), w_down.astype(BF16), gpost.reshape(1, D_MODEL))


def _layer(x2, b, s, cos, sin, p):
    t = b * s
    u = _inproj(x2, b, s, p["attn_pre_gain"], _inproj_weight(p["w_in"]), cos, sin)
    r3 = lambda a: a.reshape(b, s, a.shape[-1])

    gates = u["gnt"].reshape(b, KV_HEADS, GATE_ROWS, s)
    sinks = jnp.repeat(p["attn_sinks"].astype(F32) * LOG2E, BLOCK).reshape(KV_HEADS, 1, GROUP * BLOCK)

    o_a = _band_attention(r3(u["qa"]), r3(u["ka"]), u["vat"], window=A_WINDOW, sinks=sinks, out_dtype=BF16)

    ncp = s // CMP_STRIDE
    flat = u["kvc"].reshape(2, b, ncp, CMP_STRIDE * KV_WIDTH)
    cmp = _compress(flat, _compress_weights(p["cmp_pos_emb"], p["cmp_w1"], p["cmp_b1"], p["cmp_w2"]))
    o_c, sel = _cmp_attention(r3(u["qbraw"]), cmp[0], jnp.transpose(cmp[1], (0, 2, 1)), gates)
    o_cs = _sel_attention(r3(u["qbrot"]), r3(u["ks"]), u["vst"], sel, gates, o_c)
    o_b = _band_attention(r3(u["qbrot"]), r3(u["kw"]), u["vwt"], window=B_WINDOW,
                          gates=gates, gate_branch=2, prev=o_cs)

    x2 = _merge(o_a.reshape(t, Q_WIDTH), o_b.reshape(t, Q_WIDTH), u["ga"], u["gb"], x2,
                p["w_branch_a"], p["w_branch_b"], p["w_out"], p["attn_post_gain"])
    return _ffn(x2, s, p["ffn_pre_gain"], p["ffn_w_up"], p["ffn_conv_w"], p["ffn_conv_b"], p["ffn_w_down"],
                p["ffn_post_gain"])


def kernel(x, positions, attn_pre_gain, attn_post_gain, ffn_pre_gain, ffn_post_gain, w_in, attn_sinks,
           cmp_pos_emb, cmp_w1, cmp_b1, cmp_w2, w_branch_a, w_branch_b, w_out, ffn_w_up, ffn_conv_w,
           ffn_conv_b, ffn_w_down):
    b, s, d = x.shape
    params = dict(attn_pre_gain=attn_pre_gain, attn_post_gain=attn_post_gain, ffn_pre_gain=ffn_pre_gain,
                  ffn_post_gain=ffn_post_gain, w_in=w_in, attn_sinks=attn_sinks, cmp_pos_emb=cmp_pos_emb,
                  cmp_w1=cmp_w1, cmp_b1=cmp_b1, cmp_w2=cmp_w2, w_branch_a=w_branch_a, w_branch_b=w_branch_b,
                  w_out=w_out, ffn_w_up=ffn_w_up, ffn_conv_w=ffn_conv_w, ffn_conv_b=ffn_conv_b,
                  ffn_w_down=ffn_w_down)
    cos, sin = _rope_tables(positions)
    x2 = x.reshape(b * s, d)
    for layer in range(attn_pre_gain.shape[0]):
        x2 = _layer(x2, b, s, cos, sin, {k: v[layer] for k, v in params.items()})
    return x2.reshape(b, s, d)
```

```python
import functools

import numpy as np
import jax
import jax.numpy as jnp
from jax import lax
from jax.experimental import pallas as pl
from jax.experimental.pallas import tpu as pltpu

F32 = jnp.float32
BF16 = jnp.bfloat16

D_MODEL = 1024
HEAD_DIM = 64
BLOCK = 128
Q_HEADS = 8
KV_HEADS = 2
GROUP = Q_HEADS // KV_HEADS
A_WINDOW = 128
B_WINDOW = 512
CMP_LEN = 32
CMP_STRIDE = 16
CMP_HIDDEN = 128
SLC_LEN = 64
N_SLC = 16
N_LOCAL_SLC = 2
D_FF = 2816
ROPE_THETA = 10000.0
EPS = 1e-6
TINY = 1e-30
NEG_INF = float("-inf")
LOG2E = 1.4426950408889634
Q_SCALE = HEAD_DIM ** -0.5 * LOG2E
NEG_BIG = -1e30

LANES = 128
VMEM_LIMIT = 56 * 1024 * 1024

Q_WIDTH = Q_HEADS * HEAD_DIM
KV_WIDTH = KV_HEADS * HEAD_DIM
KV_DUP = 2 * KV_WIDTH
N_GATES = 3 * Q_HEADS
GATE_ROWS = 16

SEL_TILE = 1024
BAND_QBLOCKS = 8
FFN_CHUNK = 256
FFN_HALO = 16


def _params(sem, flags=None):
    return pltpu.CompilerParams(dimension_semantics=sem, vmem_limit_bytes=VMEM_LIMIT, flags=flags)


def _rms(x, gain):
    return x * lax.rsqrt(jnp.mean(x * x, axis=-1, keepdims=True) + EPS) * gain


def _gelu_tanh(x):
    return 0.5 * x * (1.0 + jnp.tanh(0.7978845608028654 * (x + 0.044715 * (x * x * x))))


def _sigmoid(x):
    return 1.0 / (1.0 + jnp.exp(-x))


def _rope_table_kernel(pos_ref, invf_ref, cos_ref, sin_ref):
    ang = pos_ref[...].astype(F32) * invf_ref[...]
    lane = lax.broadcasted_iota(jnp.int32, ang.shape, 1)
    sign = jnp.where((lane & (HEAD_DIM - 1)) < HEAD_DIM // 2, -1.0, 1.0).astype(F32)
    cos_ref[...] = jnp.cos(ang)
    sin_ref[...] = jnp.sin(ang) * sign


def _rope_tables(positions):
    t = positions.size
    tm = min(1024, t)
    half = HEAD_DIM // 2
    inv_freq = ROPE_THETA ** (-jnp.arange(half, dtype=F32) / half)
    invf = jnp.tile(inv_freq, LANES // half).reshape(1, LANES)
    return pl.pallas_call(
        _rope_table_kernel,
        grid=(t // tm,),
        in_specs=[pl.BlockSpec((tm, 1), lambda i: (i, 0)),
                  pl.BlockSpec((1, LANES), lambda i: (0, 0))],
        out_specs=[pl.BlockSpec((tm, LANES), lambda i: (i, 0))] * 2,
        out_shape=[jax.ShapeDtypeStruct((t, LANES), F32)] * 2,
        compiler_params=_params(("parallel",)),
        name="rope_tables",
    )(positions.reshape(t, 1), invf)


def _rope(u, cos, sin):
    lane = lax.broadcasted_iota(jnp.int32, cos.shape, 1)
    first_half = (lane & (HEAD_DIM - 1)) < HEAD_DIM // 2
    outs = []
    for c in range(u.shape[1] // LANES):
        x = u[:, c * LANES:(c + 1) * LANES]
        partner = jnp.where(first_half,
                            pltpu.roll(x, LANES - HEAD_DIM // 2, axis=1),
                            pltpu.roll(x, HEAD_DIM // 2, axis=1))
        outs.append(x * cos + partner * sin)
    return outs[0] if len(outs) == 1 else jnp.concatenate(outs, axis=1)


_IN_SEGS = (("qa", Q_WIDTH), ("ka", KV_DUP), ("va", KV_WIDTH), ("qb", Q_WIDTH),
            ("kc", KV_WIDTH), ("vc", KV_WIDTH), ("ks", KV_DUP), ("vs", KV_WIDTH),
            ("kw", KV_DUP), ("vw", KV_WIDTH), ("gn", LANES), ("ga", D_MODEL), ("gb", D_MODEL))
_IN_COLS_BIG = sum(w for _, w in _IN_SEGS)


def _inproj_kernel(x_ref, gain_ref, w_ref, cos_ref, sin_ref,
                   qa_ref, ka_ref, vat_ref, qbraw_ref, qbrot_ref, kvc_ref,
                   ks_ref, vst_ref, kw_ref, vwt_ref, gnt_ref, ga_ref, gb_ref):
    h = _rms(x_ref[...], gain_ref[...]).astype(BF16)
    cos = cos_ref[...]
    sin = sin_ref[...]
    offs = {}
    off = 0
    for name, width in _IN_SEGS:
        offs[name] = (off, width)
        off += width

    def proj(name):
        o, w = offs[name]
        return jnp.dot(h, w_ref[:, o:o + w], preferred_element_type=F32)

    qa_ref[...] = (_rope(proj("qa"), cos, sin) * Q_SCALE).astype(BF16)
    ka_ref[...] = _rope(proj("ka"), cos, sin).astype(BF16)
    vat_ref[...] = proj("va").T.astype(BF16)
    qb = proj("qb")
    qbraw_ref[...] = (qb * Q_SCALE).astype(BF16)
    qbrot_ref[...] = (_rope(qb, cos, sin) * Q_SCALE).astype(BF16)
    kvc_ref[0] = proj("kc")
    kvc_ref[1] = proj("vc")
    ks_ref[...] = _rope(proj("ks"), cos, sin).astype(BF16)
    vs_t = proj("vs").T
    tm = vs_t.shape[1]
    ones_row = (lax.broadcasted_iota(jnp.int32, (SEL_V_ROWS - HEAD_DIM, tm), 0) == 0).astype(F32)
    vst_ref[...] = jnp.concatenate(
        [piece for g in range(KV_HEADS) for piece in (vs_t[g * HEAD_DIM:(g + 1) * HEAD_DIM], ones_row)],
        axis=0).astype(BF16)
    kw_ref[...] = _rope(proj("kw"), cos, sin).astype(BF16)
    vwt_ref[...] = proj("vw").T.astype(BF16)
    gnt_ref[...] = _sigmoid(proj("gn")).T[0:KV_HEADS * GATE_ROWS, :]
    ga_ref[...] = _sigmoid(proj("ga")).astype(BF16)
    gb_ref[...] = _sigmoid(proj("gb")).astype(BF16)


def _dup_heads(w):
    g0, g1 = w[:, :HEAD_DIM], w[:, HEAD_DIM:]
    return jnp.concatenate([g0, g0, g1, g1], axis=1)


def _inproj_weight(w_in):
    sizes = (Q_WIDTH, KV_WIDTH, KV_WIDTH, Q_WIDTH) + (KV_WIDTH,) * 6 + (N_GATES, D_MODEL, D_MODEL)
    offs = np.cumsum(sizes)[:-1].tolist()
    qa, ka, va, qb, kc, vc, ks, vs, kw, vw, gn, ga, gb = jnp.split(w_in, offs, axis=1)
    order = [(g * GROUP + h) * 3 + c for g in range(KV_HEADS) for c in range(3) for h in range(GROUP)]
    gn = gn[:, np.asarray(order)].reshape(D_MODEL, KV_HEADS, 3 * GROUP)
    gn = jnp.pad(gn, ((0, 0), (0, 0), (0, GATE_ROWS - 3 * GROUP))).reshape(D_MODEL, KV_HEADS * GATE_ROWS)
    gn = jnp.pad(gn, ((0, 0), (0, LANES - KV_HEADS * GATE_ROWS)))
    cols = [qa, _dup_heads(ka), va, qb, kc, vc, _dup_heads(ks), vs, _dup_heads(kw), vw, gn, ga, gb]
    return jnp.concatenate(cols, axis=1).astype(BF16)


def _inproj(x2, b, s, gain, w_big, cos, sin):
    t = x2.shape[0]
    tm = min(512, s)
    spb = s // tm
    row = lambda w: pl.BlockSpec((tm, w), lambda i: (i, 0))
    fixed = lambda r, c: pl.BlockSpec((r, c), lambda i: (0, 0), pipeline_mode=pl.Buffered(1))
    seq_on_lanes = lambda r: pl.BlockSpec((None, r, tm), lambda i: (i // spb, 0, i % spb))
    outs = [("qa", row(Q_WIDTH), (t, Q_WIDTH), BF16),
            ("ka", row(KV_DUP), (t, KV_DUP), BF16),
            ("vat", seq_on_lanes(KV_WIDTH), (b, KV_WIDTH, s), BF16),
            ("qbraw", row(Q_WIDTH), (t, Q_WIDTH), BF16),
            ("qbrot", row(Q_WIDTH), (t, Q_WIDTH), BF16),
            ("kvc", pl.BlockSpec((2, tm, KV_WIDTH), lambda i: (0, i, 0)), (2, t, KV_WIDTH), F32),
            ("ks", row(KV_DUP), (t, KV_DUP), BF16),
            ("vst", seq_on_lanes(KV_HEADS * SEL_V_ROWS), (b, KV_HEADS * SEL_V_ROWS, s), BF16),
            ("kw", row(KV_DUP), (t, KV_DUP), BF16),
            ("vwt", seq_on_lanes(KV_WIDTH), (b, KV_WIDTH, s), BF16),
            ("gnt", seq_on_lanes(KV_HEADS * GATE_ROWS), (b, KV_HEADS * GATE_ROWS, s), F32),
            ("ga", row(D_MODEL), (t, D_MODEL), BF16),
            ("gb", row(D_MODEL), (t, D_MODEL), BF16)]
    res = pl.pallas_call(
        _inproj_kernel,
        grid=(t // tm,),
        in_specs=[row(D_MODEL), fixed(1, D_MODEL), fixed(D_MODEL, _IN_COLS_BIG), row(LANES), row(LANES)],
        out_specs=[spec for _, spec, _, _ in outs],
        out_shape=[jax.ShapeDtypeStruct(shape, dt) for _, _, shape, dt in outs],
        compiler_params=_params(("parallel",)),
        name="in_proj",
    )(x2, gain.reshape(1, D_MODEL), w_big, cos, sin)
    return {name: o for (name, _, _, _), o in zip(outs, res)}


def _compress_kernel(flat_ref, pe_ref, w1t_ref, w1b_ref, b1_ref, w2_ref, o_ref):
    f = flat_ref[...]
    pe = pe_ref[...]
    top = jnp.dot((f + pe[0:1]).astype(BF16), w1t_ref[...], preferred_element_type=F32)
    bot = jnp.dot((f + pe[1:2]).astype(BF16), w1b_ref[...], preferred_element_type=F32)
    ncp = f.shape[0]
    hid = _gelu_tanh(top + pltpu.roll(bot, ncp - 1, axis=0) + b1_ref[...])
    o_ref[...] = jnp.dot(hid.astype(BF16), w2_ref[...], preferred_element_type=F32).astype(BF16)


def _compress_weights(pos_emb, w1, b1, w2):
    half = CMP_LEN // 2
    eye = jnp.eye(KV_HEADS, dtype=F32)
    w1r = w1.reshape(2, CMP_LEN, HEAD_DIM, CMP_HIDDEN)

    def big(part):
        return jnp.einsum("wtdh,gk->wtgdkh", part, eye).reshape(
            2, half * KV_WIDTH, KV_HEADS * CMP_HIDDEN).astype(BF16)

    def pe_big(part):
        return jnp.broadcast_to(part[:, :, None, :], (2, half, KV_HEADS, HEAD_DIM)).reshape(2, half * KV_WIDTH)

    pe = jnp.stack([pe_big(pos_emb[:, :half]), pe_big(pos_emb[:, half:])], axis=1)
    b1b = jnp.tile(b1, (1, KV_HEADS)).reshape(2, 1, KV_HEADS * CMP_HIDDEN)
    w2b = jnp.einsum("whd,gk,r->wghkrd", w2, eye, jnp.ones((2,), F32)).reshape(
        2, KV_HEADS * CMP_HIDDEN, KV_DUP).astype(BF16)
    return pe, big(w1r[:, :half]), big(w1r[:, half:]), b1b, w2b


def _compress(flat, cw):
    pe, w1t, w1b, b1b, w2b = cw
    _, b, ncp, fw = flat.shape
    hw = KV_HEADS * CMP_HIDDEN
    per_w = lambda *shape: pl.BlockSpec((None,) + shape, lambda w, bb: (w,) + (0,) * len(shape))
    return pl.pallas_call(
        _compress_kernel,
        grid=(2, b),
        in_specs=[pl.BlockSpec((None, None, ncp, fw), lambda w, bb: (w, bb, 0, 0)),
                  per_w(2, fw), per_w(fw, hw), per_w(fw, hw), per_w(1, hw), per_w(hw, KV_DUP)],
        out_specs=pl.BlockSpec((None, None, ncp, KV_DUP), lambda w, bb: (w, bb, 0, 0)),
        out_shape=jax.ShapeDtypeStruct((2, b, ncp, KV_DUP), BF16),
        compiler_params=_params(("parallel", "parallel")),
        name="compress",
    )(flat, pe, w1t, w1b, b1b, w2b)


def _stack_heads(q):
    lane = lax.broadcasted_iota(jnp.int32, (BLOCK, LANES), 1)
    lo = lane < HEAD_DIM
    zero = jnp.zeros((BLOCK, LANES), q.dtype)
    p0, p1 = q[:, :LANES], q[:, LANES:]
    return jnp.concatenate([jnp.where(lo, p0, zero), jnp.where(lo, zero, p0),
                            jnp.where(lo, p1, zero), jnp.where(lo, zero, p1)], axis=0)


def _unstack_out(o_t):
    lane = lax.broadcasted_iota(jnp.int32, (BLOCK, LANES), 1)
    lo = lane < HEAD_DIM
    o2 = jnp.concatenate([o_t, o_t], axis=0)
    t = [o2[:, h * BLOCK:(h + 1) * BLOCK].T for h in range(GROUP)]
    return jnp.concatenate([jnp.where(lo, t[0], t[1]), jnp.where(lo, t[2], t[3])], axis=1)


def _gate_row(g, branch):
    return jnp.concatenate([g[branch * GROUP + h:branch * GROUP + h + 1, :] for h in range(GROUP)], axis=1)


def _nt_dot(a, b):
    return lax.dot_general(a, b, (((1,), (1,)), ((), ())), preferred_element_type=F32)


def _attn_specs(s):
    nb = s // BLOCK
    q_spec = pl.BlockSpec((None, BLOCK, GROUP * HEAD_DIM), lambda b, g, i: (b, i, g))
    k_spec = pl.BlockSpec((None, s, LANES), lambda b, g, i: (b, 0, g))
    vt_spec = pl.BlockSpec((None, HEAD_DIM, s), lambda b, g, i: (b, g, 0))
    gate_spec = pl.BlockSpec((None, None, GATE_ROWS,BLOCK), lambda b, g, i: (b, g, 0, i))
    return nb, q_spec, k_spec, vt_spec, gate_spec


def _band_kernel(*refs, window, nkb, use_sink, gate_branch, has_prev):
    refs = list(refs)
    q_ref, k_ref, vt_ref = refs[:3]
    pos = 3
    g_ref = sink_ref = prev_ref = None
    if gate_branch is not None:
        g_ref = refs[pos]; pos += 1
    if use_sink:
        sink_ref = refs[pos]; pos += 1
    if has_prev:
        prev_ref = refs[pos]; pos += 1
    o_ref = refs[pos]

    length = nkb * BLOCK
    for sub in range(BAND_QBLOCKS):
        i = pl.program_id(2) * BAND_QBLOCKS + sub
        rows = slice(sub * BLOCK, (sub + 1) * BLOCK)
        start = pl.multiple_of(jnp.maximum(i - (nkb - 1), 0) * BLOCK, BLOCK)
        kb = k_ref[pl.ds(start, length), :]
        vb = vt_ref[:, pl.ds(start, length)]
        qs = _stack_heads(q_ref[rows, :])
        kpos = start + lax.broadcasted_iota(jnp.int32, (length, BLOCK), 0)
        qpos = i * BLOCK + lax.broadcasted_iota(jnp.int32, (length, BLOCK), 1)
        rel = qpos - kpos
        bias = jnp.where((rel >= 0) & (rel < window), 0.0, NEG_INF).astype(F32)
        s = _nt_dot(kb, qs) + jnp.concatenate([bias] * GROUP, axis=1)
        m = jnp.max(s, axis=0, keepdims=True)
        if use_sink:
            sink = sink_ref[...]
            m = jnp.maximum(m, sink)
        p = jnp.exp2(s - m)
        l = jnp.sum(p, axis=0, keepdims=True)
        if use_sink:
            l = l + jnp.exp2(sink - m)
        o_t = jnp.dot(vb, p.astype(BF16), preferred_element_type=F32)
        scale = 1.0 / jnp.maximum(l, TINY)
        if gate_branch is not None:
            scale = scale * _gate_row(g_ref[:, rows], gate_branch)
        out = _unstack_out(o_t * scale)
        if has_prev:
            out = out + prev_ref[rows, :]
        o_ref[rows, :] = out.astype(o_ref.dtype)


def _band_attention(q, k, vt, *, window, gates=None, gate_branch=None, sinks=None, prev=None, out_dtype=F32):
    b, s, _ = q.shape
    nb, _, k_spec, vt_spec, _ = _attn_specs(s)
    rows = BAND_QBLOCKS * BLOCK
    q_spec = pl.BlockSpec((None, rows, GROUP * HEAD_DIM), lambda b_, g, i: (b_, i, g))
    gate_spec = pl.BlockSpec((None, None, GATE_ROWS,rows), lambda b_, g, i: (b_, g, 0, i))
    nkb = -(-window // BLOCK) + 1
    ins, specs = [q, k, vt], [q_spec, k_spec, vt_spec]
    if gate_branch is not None:
        ins.append(gates); specs.append(gate_spec)
    if sinks is not None:
        ins.append(sinks)
        specs.append(pl.BlockSpec((None, 1, GROUP * BLOCK), lambda b_, g, i: (g, 0, 0)))
    if prev is not None:
        ins.append(prev); specs.append(q_spec)
    kern = functools.partial(_band_kernel, window=window, nkb=nkb, use_sink=sinks is not None,
                             gate_branch=gate_branch, has_prev=prev is not None)
    return pl.pallas_call(
        kern,
        grid=(b, KV_HEADS, nb // BAND_QBLOCKS),
        in_specs=specs,
        out_specs=q_spec,
        out_shape=jax.ShapeDtypeStruct((b, s, Q_WIDTH), out_dtype),
        compiler_params=_params(("parallel", "parallel", "parallel")),
        name=f"band_attn_w{window}",
    )(*ins)


def _cmp_kernel(q_ref, kc_ref, vct_ref, g_ref, mt_ref, o_ref, sel_ref, *, n_sel):
    step = pl.program_id(2)
    steps = pl.num_programs(2)
    ncp_all, ns_all = kc_ref.shape[0], mt_ref.shape[0]

    def prefix(ncp, ns):
        for sub in range(CMP_QBLOCKS):
            _cmp_block(step * CMP_QBLOCKS + sub, slice(sub * BLOCK, (sub + 1) * BLOCK),
                       q_ref, kc_ref, vct_ref, g_ref, mt_ref, o_ref, sel_ref, n_sel=n_sel, ncp=ncp, ns=ns)

    for k in range(CMP_PREFIXES):
        ncp = ncp_all * (k + 1) // CMP_PREFIXES
        ns = min(ns_all, -(-(ns_all * (k + 1) // CMP_PREFIXES) // LANES) * LANES)
        pl.when(step * CMP_PREFIXES // steps == k)(functools.partial(prefix, ncp, ns))


def _cmp_block(i, rows, q_ref, kc_ref, vct_ref, g_ref, mt_ref, o_ref, sel_ref, *, n_sel, ncp, ns):
    qs = _stack_heads(q_ref[rows, :])
    c = lax.broadcasted_iota(jnp.int32, (ncp, BLOCK), 0)
    t = i * BLOCK + lax.broadcasted_iota(jnp.int32, (1, BLOCK), 1)
    last = (t - (CMP_LEN - 1)) >> (CMP_STRIDE.bit_length() - 1)
    bias = jnp.where(c <= last, 0.0, NEG_INF).astype(F32)
    s = _nt_dot(kc_ref[0:ncp, :], qs) + jnp.concatenate([bias] * GROUP, axis=1)
    m = jnp.max(s, axis=0, keepdims=True)
    m = jnp.where(m == NEG_INF, 0.0, m)
    p = jnp.exp2(s - m)
    l = jnp.sum(p, axis=0, keepdims=True)
    p = p * (1.0 / jnp.maximum(l, TINY))
    o_t = jnp.dot(vct_ref[:, 0:ncp], p.astype(BF16), preferred_element_type=F32)
    o_ref[rows, :] = _unstack_out(o_t * _gate_row(g_ref[:, rows], 0))

    imp = p[:, 0:BLOCK]
    for h in range(1, GROUP):
        imp = imp + p[:, h * BLOCK:(h + 1) * BLOCK]
    hi = imp.astype(BF16)
    r1 = imp - hi.astype(F32)
    mid = r1.astype(BF16)
    lo = (r1 - mid.astype(F32)).astype(BF16)
    mt = mt_ref[0:ns, 0:ncp]
    imp_s = (jnp.dot(mt, hi, preferred_element_type=F32) + jnp.dot(mt, mid, preferred_element_type=F32)
             + jnp.dot(mt, lo, preferred_element_type=F32))

    j =lax.broadcasted_iota(jnp.int32, imp_s.shape, 0)
    tq = i * BLOCK + lax.broadcasted_iota(jnp.int32, imp_s.shape, 1)
    cur = tq >> (SLC_LEN.bit_length() - 1)
    valid = j <= cur
    forced = (j == 0) | (valid & (j > cur - N_LOCAL_SLC))
    work = jnp.where(valid & jnp.logical_not(forced), imp_s, NEG_INF)
    sel_bias = jnp.where(forced, 0.0, NEG_BIG).astype(F32)
    for _ in range(n_sel - 1 - N_LOCAL_SLC):
        mx = jnp.max(work, axis=0, keepdims=True)
        first = jnp.min(jnp.where(work == mx, j, ns), axis=0, keepdims=True)
        pick = j == first
        sel_bias = jnp.where(pick, 0.0, sel_bias)
        work = jnp.where(pick, NEG_INF, work)
    sel_bias = jnp.where(valid, sel_bias, NEG_BIG)
    sel_q = sel_bias.T.astype(BF16)
    pad = sel_ref.shape[1] - ns
    sel_ref[rows, :] = jnp.concatenate([sel_q, jnp.full((BLOCK, pad), NEG_BIG, BF16)], axis=1)


def _overlap_map(ns, ncp):
    ratio_c = CMP_LEN // CMP_STRIDE
    ratio_s = SLC_LEN // CMP_STRIDE
    m = np.zeros((ns, ncp), np.float32)
    for jj in range(ns):
        for a in range(ratio_s):
            for n in range(ratio_c):
                c = ratio_s * jj + a - n
                if 0 <= c < ncp - ratio_c + 1:
                    m[jj, c] += 1.0
    return jnp.asarray(m, BF16)


def _cmp_attention(q_raw, kc, vct, gates):
    b, s, _ = q_raw.shape
    nb = s // BLOCK
    rows = CMP_QBLOCKS * BLOCK
    assert nb % (CMP_QBLOCKS * CMP_PREFIXES) == 0
    q_spec = pl.BlockSpec((None, rows, GROUP * HEAD_DIM), lambda b_, g, i: (b_, i, g))
    gate_spec = pl.BlockSpec((None, None, GATE_ROWS,rows), lambda b_, g, i: (b_, g, 0, i))
    ncp = s // CMP_STRIDE
    ns = s // SLC_LEN
    assert ns >= N_SLC
    nsp = -(-ns // LANES) * LANES + LANES
    mt = _overlap_map(ns, ncp)
    kern = functools.partial(_cmp_kernel, n_sel=N_SLC)
    return pl.pallas_call(
        kern,
        grid=(b, KV_HEADS, nb // CMP_QBLOCKS),
        in_specs=[q_spec,
                  pl.BlockSpec((None, ncp, LANES), lambda b_, g, i: (b_, 0, g)),
                  pl.BlockSpec((None, HEAD_DIM, ncp), lambda b_, g, i: (b_, 2 * g, 0)),
                  gate_spec,
                  pl.BlockSpec((ns, ncp), lambda b_, g, i: (0, 0))],
        out_specs=[q_spec, pl.BlockSpec((None, None, rows, nsp), lambda b_, g, i: (b_, g, i, 0))],
        out_shape=[jax.ShapeDtypeStruct((b, s, Q_WIDTH), F32),
                   jax.ShapeDtypeStruct((b, KV_HEADS, s, nsp), BF16)],
        compiler_params=_params(("parallel", "parallel", "parallel")),
        name="cmp_attn_topk",
    )(q_raw, kc, vct, gates, mt)


CMP_PREFIXES = 4
CMP_QBLOCKS = 4
SEL_BLOCKS_PER_TILE = SEL_TILE // SLC_LEN
SEL_TILES_PER_LANES = LANES // SEL_BLOCKS_PER_TILE
SEL_DIAG_VARIANTS = 4
SEL_V_ROWS = HEAD_DIM + 16


def _sel_kernel(q_ref, k_ref, vt_ref, sel_ref, oh_ref, g_ref, prev_ref, o_ref,
                s0_sc, sa_sc, sb_sc, mxa_sc, mxb_sc, ra_sc, rb_sc, m_sc, acc_sc):
    i = pl.program_id(2)
    tk = SEL_TILE
    qs = _stack_heads(q_ref[...])
    diag = i // (tk // BLOCK)
    masked_lanes = sel_ref.shape[1] - LANES
    bufs = ((sa_sc, mxa_sc, ra_sc), (sb_sc, mxb_sc, rb_sc))

    def scores(tile, lane0):
        k0 = pl.multiple_of(tile * tk, tk)
        feat = sel_ref[:, pl.ds(pl.multiple_of(lane0, LANES), LANES)]
        qa = jnp.concatenate([qs, jnp.concatenate([feat] * GROUP, axis=0)], axis=1)
        ka = jnp.concatenate([k_ref[pl.ds(k0, tk), :], oh_ref[tile % SEL_TILES_PER_LANES]], axis=1)
        return _nt_dot(ka, qa)

    def values(tile):
        return vt_ref[:, pl.ds(pl.multiple_of(tile * tk, tk), tk)]

    def tile_or_pad(tile):
        return jnp.minimum(tile, jnp.maximum(diag - 1, 0))

    def produce(tile, buf):
        tc = tile_or_pad(tile)
        lane0 = jnp.where(tile < diag, (tc // SEL_TILES_PER_LANES) * LANES, masked_lanes)
        sn = scores(tc, lane0)
        ref = m_sc[...]
        buf[0][...] = (sn - ref).astype(BF16)
        buf[1][...] = jnp.max(sn, axis=0, keepdims=True)
        buf[2][...] = ref

    def consume(tile, buf):
        m_old = m_sc[...]
        m_new = jnp.maximum(m_old, buf[1][...])
        alpha = jnp.exp2(m_old - m_new)
        shift = (buf[2][...] - m_new).astype(BF16)
        p = jnp.exp2(buf[0][...] + shift)
        acc_sc[...] = alpha * acc_sc[...] + jnp.dot(values(tile_or_pad(tile)), p, preferred_element_type=F32)
        m_sc[...] = m_new

    def diagonal(rows):
        k0 = pl.multiple_of(diag * tk, tk)
        lane0 = pl.multiple_of((diag // SEL_TILES_PER_LANES) * LANES, LANES)
        qa = jnp.concatenate([qs, jnp.concatenate([sel_ref[:, pl.ds(lane0, LANES)]] * GROUP, axis=0)], axis=1)
        ka = jnp.concatenate([k_ref[pl.ds(k0, rows), :], oh_ref[diag % SEL_TILES_PER_LANES, 0:rows, :]], axis=1)
        kpos = k0 + lax.broadcasted_iota(jnp.int32, (rows, BLOCK), 0)
        qpos = i * BLOCK + lax.broadcasted_iota(jnp.int32, (rows, BLOCK), 1)
        causal = jnp.where(kpos <= qpos, 0.0, NEG_BIG).astype(F32)
        s0 = _nt_dot(ka, qa) + jnp.concatenate([causal] * GROUP, axis=1)
        m0 = jnp.max(s0, axis=0, keepdims=True)
        m_sc[...] = m0
        s0_sc[0:rows, :] = s0
        produce(0, bufs[0])
        p0 = jnp.exp2(s0_sc[0:rows, :] - m0).astype(BF16)
        acc_sc[...] = jnp.dot(vt_ref[:, pl.ds(k0, rows)], p0, preferred_element_type=F32)

    blocks_per_tile = tk // BLOCK
    for v in range(SEL_DIAG_VARIANTS):
        rows = tk * (v + 1) // SEL_DIAG_VARIANTS
        pl.when((i % blocks_per_tile) * SEL_DIAG_VARIANTS // blocks_per_tile == v)(
            functools.partial(diagonal, rows))

    def body(pair, carry):
        tile = 2 * pair
        produce(tile + 1, bufs[1])
        consume(tile, bufs[0])
        produce(tile + 2, bufs[0])
        consume(tile + 1, bufs[1])
        return carry

    lax.fori_loop(0, (diag + 1) // 2, body, 0)
    acc = acc_sc[...]
    scale = _gate_row(g_ref[...], 1) / jnp.maximum(acc[HEAD_DIM:HEAD_DIM + 1, :], TINY)
    o_ref[...] = _unstack_out(acc[0:HEAD_DIM, :] * scale) + prev_ref[...]


def _block_onehots():
    jm = np.arange(SEL_TILES_PER_LANES)[:, None, None]
    r = np.arange(SEL_TILE)[None, :, None]
    c = np.arange(LANES)[None, None, :]
    return jnp.asarray(c == jm * SEL_BLOCKS_PER_TILE + r // SLC_LEN, BF16)


def _sel_attention(q_rot, ks, vst, sel, gates, prev):
    b, s, _ = q_rot.shape
    nb, q_spec, k_spec, _, gate_spec = _attn_specs(s)
    vt_spec = pl.BlockSpec((None, SEL_V_ROWS, s), lambda b_, g, i: (b_, g, 0))
    nsp = sel.shape[-1]
    cols = GROUP * BLOCK
    return pl.pallas_call(
        _sel_kernel,
        grid=(b, KV_HEADS, nb),
        in_specs=[q_spec, k_spec, vt_spec,
                  pl.BlockSpec((None, None, BLOCK, nsp), lambda b_, g, i: (b_, g, i, 0)),
                  pl.BlockSpec((SEL_TILES_PER_LANES, SEL_TILE, LANES), lambda b_, g, i: (0, 0, 0)),
                  gate_spec, q_spec],
        out_specs=q_spec,
        out_shape=jax.ShapeDtypeStruct((b, s, Q_WIDTH), F32),
        scratch_shapes=[pltpu.VMEM((SEL_TILE, cols), F32),
                        pltpu.VMEM((SEL_TILE, cols), BF16), pltpu.VMEM((SEL_TILE, cols), BF16),
                        pltpu.VMEM((1, cols), F32), pltpu.VMEM((1, cols), F32),
                        pltpu.VMEM((1, cols), F32), pltpu.VMEM((1, cols), F32),
                        pltpu.VMEM((1, cols), F32),
                        pltpu.VMEM((SEL_V_ROWS, cols), F32)],
        compiler_params=_params(("parallel", "parallel", "parallel")),
        name="sel_attn",
    )(q_rot, ks, vst, sel, _block_onehots(), gates, prev)


def _merge_kernel(oa_ref, ob_ref, ga_ref, gb_ref, x_ref, wa_ref, wb_ref, wo_ref, gain_ref, o_ref):
    ya = jnp.dot(oa_ref[...], wa_ref[...], preferred_element_type=F32)
    yb = jnp.dot(ob_ref[...].astype(BF16), wb_ref[...], preferred_element_type=F32)
    mixed = ga_ref[...].astype(F32) * ya + gb_ref[...].astype(F32) * yb
    z = jnp.dot(mixed.astype(BF16), wo_ref[...], preferred_element_type=F32)
    o_ref[...] = x_ref[...] + _rms(z, gain_ref[...])


def _merge(o_a, o_b, ga, gb, x2, w_a, w_b, w_o, gain):
    t = x2.shape[0]
    tm = 512
    row = lambda w: pl.BlockSpec((tm, w), lambda i: (i, 0))
    fixed = lambda r, c: pl.BlockSpec((r, c), lambda i: (0, 0))
    return pl.pallas_call(
        _merge_kernel,
        grid=(t // tm,),
        in_specs=[row(Q_WIDTH), row(Q_WIDTH), row(D_MODEL), row(D_MODEL), row(D_MODEL),
                  fixed(Q_WIDTH, D_MODEL), fixed(Q_WIDTH, D_MODEL), fixed(D_MODEL, D_MODEL), fixed(1, D_MODEL)],
        out_specs=row(D_MODEL),
        out_shape=jax.ShapeDtypeStruct((t, D_MODEL), F32),
        compiler_params=_params(("parallel",)),
        name="merge_out_proj",
    )(o_a, o_b, ga, gb, x2, w_a.astype(BF16), w_b.astype(BF16), w_o.astype(BF16), gain.reshape(1, D_MODEL))


def _ffn_kernel(xp_ref, x_ref, gpre_ref, wu_ref, cw_ref, cb_ref, wd_ref, gpost_ref, o_ref,
                h_sc, ua_sc, ub_sc, g_sc, *, tm, tiles_per_seq, nck):
    i = pl.program_id(0)
    gpre = gpre_ref[...]
    keep = (i % tiles_per_seq != 0).astype(F32)
    h_sc[0:FFN_HALO, :] = (_rms(xp_ref[...], gpre) * keep).astype(BF16)
    h_sc[FFN_HALO:, :] = _rms(x_ref[...], gpre).astype(BF16)

    def produce(n, u_sc):
        h = h_sc[...]
        u_sc[0] = jnp.dot(h, wu_ref[n], preferred_element_type=F32)
        u_sc[1] = jnp.dot(h, wu_ref[nck + n], preferred_element_type=F32)

    def consume(n, u_sc):
        def conv(part, col):
            cw = cw_ref[col]
            return (cb_ref[col] + cw[0:1] * u_sc[part, pl.ds(FFN_HALO - 2, tm), :]
                    + cw[1:2] * u_sc[part, pl.ds(FFN_HALO - 1, tm), :]
                    + cw[2:3] * u_sc[part, pl.ds(FFN_HALO, tm), :])

        g_sc[n] = (_gelu_tanh(conv(0, n)) * conv(1, nck + n)).astype(BF16)

    bufs = (ua_sc, ub_sc)
    produce(0, bufs[0])
    for n in range(nck):
        if n + 1 < nck:
            produce(n + 1, bufs[(n + 1) % 2])
        consume(n, bufs[n % 2])
    g = jnp.concatenate([g_sc[n] for n in range(nck)], axis=1)
    z = jnp.dot(g, wd_ref[...], preferred_element_type=F32)
    o_ref[...] = x_ref[...] + _rms(z, gpost_ref[...])


def _ffn(x2, seq, gpre, w_up, conv_w, conv_b, w_down, gpost):
    t = x2.shape[0]
    tm = min(512, seq)
    ck = FFN_CHUNK
    nck = D_FF // ck
    assert nck % 2 == 1
    kern = functools.partial(_ffn_kernel, tm=tm, tiles_per_seq=seq // tm, nck=nck)
    halo_blocks = tm // FFN_HALO
    chunks = lambda a: jnp.transpose(a.reshape(a.shape[0], 2 * nck, ck), (1, 0, 2))
    whole = lambda *shape: pl.BlockSpec(shape, lambda i: (0,) * len(shape), pipeline_mode=pl.Buffered(1))
    return pl.pallas_call(
        kern,
        grid=(t // tm,),
        in_specs=[pl.BlockSpec((FFN_HALO, D_MODEL), lambda i: (jnp.maximum(i * halo_blocks - 1, 0), 0)),
                  pl.BlockSpec((tm, D_MODEL), lambda i: (i, 0)),
                  whole(1, D_MODEL),
                  whole(2 * nck, D_MODEL, ck),
                  whole(2 * nck, 3, ck),
                  whole(2 * nck, 1, ck),
                  whole(D_FF, D_MODEL),
                  whole(1, D_MODEL)],
        out_specs=pl.BlockSpec((tm, D_MODEL), lambda i: (i, 0)),
        out_shape=jax.ShapeDtypeStruct((t, D_MODEL), F32),
        scratch_shapes=[pltpu.VMEM((tm + FFN_HALO, D_MODEL), BF16),
                        pltpu.VMEM((2, tm + FFN_HALO, ck), F32),
                        pltpu.VMEM((2, tm + FFN_HALO, ck), F32),
                        pltpu.VMEM((nck, tm, ck), BF16)],
        compiler_params=_params(("parallel",)),
        name="conv_ffn",
    )(x2, x2, gpre.reshape(1, D_MODEL), chunks(w_up.astype(BF16)), chunks(conv_w),
      chunks(conv_b.reshape(1, 2 * D_FF)), w_down.astype(BF16), gpost.reshape(1, D_MODEL))


def _layer(x2, b, s, cos, sin, p):
    t = b * s
    u = _inproj(x2, b, s, p["attn_pre_gain"], _inproj_weight(p["w_in"]), cos, sin)
    r3 = lambda a: a.reshape(b, s, a.shape[-1])

    gates = u["gnt"].reshape(b, KV_HEADS, GATE_ROWS, s)
    sinks = jnp.repeat(p["attn_sinks"].astype(F32) * LOG2E, BLOCK).reshape(KV_HEADS, 1, GROUP * BLOCK)

    o_a = _band_attention(r3(u["qa"]), r3(u["ka"]), u["vat"], window=A_WINDOW, sinks=sinks, out_dtype=BF16)

    ncp = s // CMP_STRIDE
    flat = u["kvc"].reshape(2, b, ncp, CMP_STRIDE * KV_WIDTH)
    cmp = _compress(flat, _compress_weights(p["cmp_pos_emb"], p["cmp_w1"], p["cmp_b1"], p["cmp_w2"]))
    o_c, sel = _cmp_attention(r3(u["qbraw"]), cmp[0], jnp.transpose(cmp[1], (0, 2, 1)), gates)
    o_cs = _sel_attention(r3(u["qbrot"]), r3(u["ks"]), u["vst"], sel, gates, o_c)
    o_b = _band_attention(r3(u["qbrot"]), r3(u["kw"]), u["vwt"], window=B_WINDOW,
                          gates=gates, gate_branch=2, prev=o_cs)

    x2 = _merge(o_a.reshape(t, Q_WIDTH), o_b.reshape(t, Q_WIDTH), u["ga"], u["gb"], x2,
                p["w_branch_a"], p["w_branch_b"], p["w_out"], p["attn_post_gain"])
    return _ffn(x2, s, p["ffn_pre_gain"], p["ffn_w_up"], p["ffn_conv_w"], p["ffn_conv_b"], p["ffn_w_down"],
                p["ffn_post_gain"])


def kernel(x, positions, attn_pre_gain, attn_post_gain, ffn_pre_gain, ffn_post_gain, w_in, attn_sinks,
           cmp_pos_emb, cmp_w1, cmp_b1, cmp_w2, w_branch_a, w_branch_b, w_out, ffn_w_up, ffn_conv_w,
           ffn_conv_b, ffn_w_down):
    b, s, d = x.shape
    params = dict(attn_pre_gain=attn_pre_gain, attn_post_gain=attn_post_gain, ffn_pre_gain=ffn_pre_gain,
                  ffn_post_gain=ffn_post_gain, w_in=w_in, attn_sinks=attn_sinks, cmp_pos_emb=cmp_pos_emb,
                  cmp_w1=cmp_w1, cmp_b1=cmp_b1, cmp_w2=cmp_w2, w_branch_a=w_branch_a, w_branch_b=w_branch_b,
                  w_out=w_out, ffn_w_up=ffn_w_up, ffn_conv_w=ffn_conv_w, ffn_conv_b=ffn_conv_b,
                  ffn_w_down=ffn_w_down)
    cos, sin = _rope_tables(positions)
    x2 = x.reshape(b * s, d)
    for layer in range(attn_pre_gain.shape[0]):
        x2 = _layer(x2, b, s, cos, sin, {k: v[layer] for k, v in params.items()})
    return x2.reshape(b, s, d)
```

```python
import functools

import numpy as np
import jax
import jax.numpy as jnp
from jax import lax
from jax.experimental import pallas as pl
from jax.experimental.pallas import tpu as pltpu

F32 = jnp.float32
BF16 = jnp.bfloat16

D_MODEL = 1024
HEAD_DIM = 64
BLOCK = 128
Q_HEADS = 8
KV_HEADS = 2
GROUP = Q_HEADS // KV_HEADS
A_WINDOW = 128
B_WINDOW = 512
CMP_LEN = 32
CMP_STRIDE = 16
CMP_HIDDEN = 128
SLC_LEN = 64
N_SLC = 16
N_LOCAL_SLC = 2
D_FF = 2816
ROPE_THETA = 10000.0
EPS = 1e-6
TINY = 1e-30
NEG_INF = float("-inf")
LOG2E = 1.4426950408889634
Q_SCALE = HEAD_DIM ** -0.5 * LOG2E
NEG_BIG = -1e30

LANES = 128
VMEM_LIMIT = 56 * 1024 * 1024

Q_WIDTH = Q_HEADS * HEAD_DIM
KV_WIDTH = KV_HEADS * HEAD_DIM
KV_DUP = 2 * KV_WIDTH
N_GATES = 3 * Q_HEADS
GATE_ROWS = 16

SEL_TILE = 1024
BAND_QBLOCKS = 8
FFN_CHUNK = 256
FFN_HALO = 16


def _params(sem, flags=None):
    return pltpu.CompilerParams(dimension_semantics=sem, vmem_limit_bytes=VMEM_LIMIT, flags=flags)


def _rms(x, gain):
    return x * lax.rsqrt(jnp.mean(x * x, axis=-1, keepdims=True) + EPS) * gain


def _gelu_tanh(x):
    return 0.5 * x * (1.0 + jnp.tanh(0.7978845608028654 * (x + 0.044715 * (x * x * x))))


def _sigmoid(x):
    return 1.0 / (1.0 + jnp.exp(-x))


def _rope_table_kernel(pos_ref, invf_ref, cos_ref, sin_ref):
    ang = pos_ref[...].astype(F32) * invf_ref[...]
    lane = lax.broadcasted_iota(jnp.int32, ang.shape, 1)
    sign = jnp.where((lane & (HEAD_DIM - 1)) < HEAD_DIM // 2, -1.0, 1.0).astype(F32)
    cos_ref[...] = jnp.cos(ang)
    sin_ref[...] = jnp.sin(ang) * sign


def _rope_tables(positions):
    t = positions.size
    tm = min(1024, t)
    half = HEAD_DIM // 2
    inv_freq = ROPE_THETA ** (-jnp.arange(half, dtype=F32) / half)
    invf = jnp.tile(inv_freq, LANES // half).reshape(1, LANES)
    return pl.pallas_call(
        _rope_table_kernel,
        grid=(t // tm,),
        in_specs=[pl.BlockSpec((tm, 1), lambda i: (i, 0)),
                  pl.BlockSpec((1, LANES), lambda i: (0, 0))],
        out_specs=[pl.BlockSpec((tm, LANES), lambda i: (i, 0))] * 2,
        out_shape=[jax.ShapeDtypeStruct((t, LANES), F32)] * 2,
        compiler_params=_params(("parallel",)),
        name="rope_tables",
    )(positions.reshape(t, 1), invf)


def _rope(u, cos, sin):
    lane = lax.broadcasted_iota(jnp.int32, cos.shape, 1)
    first_half = (lane & (HEAD_DIM - 1)) < HEAD_DIM // 2
    outs = []
    for c in range(u.shape[1] // LANES):
        x = u[:, c * LANES:(c + 1) * LANES]
        partner = jnp.where(first_half,
                            pltpu.roll(x, LANES - HEAD_DIM // 2, axis=1),
                            pltpu.roll(x, HEAD_DIM // 2, axis=1))
        outs.append(x * cos + partner * sin)
    return outs[0] if len(outs) == 1 else jnp.concatenate(outs, axis=1)


_IN_SEGS = (("qa", Q_WIDTH), ("ka", KV_DUP), ("va", KV_WIDTH), ("qb", Q_WIDTH),
            ("kc", KV_WIDTH), ("vc", KV_WIDTH), ("ks", KV_DUP), ("vs", KV_WIDTH),
            ("kw", KV_DUP), ("vw", KV_WIDTH), ("gn", LANES), ("ga", D_MODEL), ("gb", D_MODEL))
_IN_COLS_BIG = sum(w for _, w in _IN_SEGS)


def _inproj_kernel(x_ref, gain_ref, w_ref, cos_ref, sin_ref,
                   qa_ref, ka_ref, vat_ref, qbraw_ref, qbrot_ref, kvc_ref,
                   ks_ref, vst_ref, kw_ref, vwt_ref, gnt_ref, ga_ref, gb_ref):
    h = _rms(x_ref[...], gain_ref[...]).astype(BF16)
    cos = cos_ref[...]
    sin = sin_ref[...]
    offs = {}
    off = 0
    for name, width in _IN_SEGS:
        offs[name] = (off, width)
        off += width

    def proj(name):
        o, w = offs[name]
        return jnp.dot(h, w_ref[:, o:o + w], preferred_element_type=F32)

    qa_ref[...] = (_rope(proj("qa"), cos, sin) * Q_SCALE).astype(BF16)
    ka_ref[...] = _rope(proj("ka"), cos, sin).astype(BF16)
    vat_ref[...] = proj("va").T.astype(BF16)
    qb = proj("qb")
    qbraw_ref[...] = (qb * Q_SCALE).astype(BF16)
    qbrot_ref[...] = (_rope(qb, cos, sin) * Q_SCALE).astype(BF16)
    kvc_ref[0] = proj("kc")
    kvc_ref[1] = proj("vc")
    ks_ref[...] = _rope(proj("ks"), cos, sin).astype(BF16)
    vs_t = proj("vs").T
    tm = vs_t.shape[1]
    ones_row = (lax.broadcasted_iota(jnp.int32, (SEL_V_ROWS - HEAD_DIM, tm), 0) == 0).astype(F32)
    vst_ref[...] = jnp.concatenate(
        [piece for g in range(KV_HEADS) for piece in (vs_t[g * HEAD_DIM:(g + 1) * HEAD_DIM], ones_row)],
        axis=0).astype(BF16)
    kw_ref[...] = _rope(proj("kw"), cos, sin).astype(BF16)
    vwt_ref[...] = proj("vw").T.astype(BF16)
    gnt_ref[...] = _sigmoid(proj("gn")).T[0:KV_HEADS * GATE_ROWS, :]
    ga_ref[...] = _sigmoid(proj("ga")).astype(BF16)
    gb_ref[...] = _sigmoid(proj("gb")).astype(BF16)


def _dup_heads(w):
    g0, g1 = w[:, :HEAD_DIM], w[:, HEAD_DIM:]
    return jnp.concatenate([g0, g0, g1, g1], axis=1)


def _inproj_weight(w_in):
    sizes = (Q_WIDTH, KV_WIDTH, KV_WIDTH, Q_WIDTH) + (KV_WIDTH,) * 6 + (N_GATES, D_MODEL, D_MODEL)
    offs = np.cumsum(sizes)[:-1].tolist()
    qa, ka, va, qb, kc, vc, ks, vs, kw, vw, gn, ga, gb = jnp.split(w_in, offs, axis=1)
    order = [(g * GROUP + h) * 3 + c for g in range(KV_HEADS) for c in range(3) for h in range(GROUP)]
    gn = gn[:, np.asarray(order)].reshape(D_MODEL, KV_HEADS, 3 * GROUP)
    gn = jnp.pad(gn, ((0, 0), (0, 0), (0, GATE_ROWS - 3 * GROUP))).reshape(D_MODEL, KV_HEADS * GATE_ROWS)
    gn = jnp.pad(gn, ((0, 0), (0, LANES - KV_HEADS * GATE_ROWS)))
    cols = [qa, _dup_heads(ka), va, qb, kc, vc, _dup_heads(ks), vs, _dup_heads(kw), vw, gn, ga, gb]
    return jnp.concatenate(cols, axis=1).astype(BF16)


def _inproj(x2, b, s, gain, w_big, cos, sin):
    t = x2.shape[0]
    tm = min(512, s)
    spb = s // tm
    row = lambda w: pl.BlockSpec((tm, w), lambda i: (i, 0))
    fixed = lambda r, c: pl.BlockSpec((r, c), lambda i: (0, 0), pipeline_mode=pl.Buffered(1))
    seq_on_lanes = lambda r: pl.BlockSpec((None, r, tm), lambda i: (i // spb, 0, i % spb))
    outs = [("qa", row(Q_WIDTH), (t, Q_WIDTH), BF16),
            ("ka", row(KV_DUP), (t, KV_DUP), BF16),
            ("vat", seq_on_lanes(KV_WIDTH), (b, KV_WIDTH, s), BF16),
            ("qbraw", row(Q_WIDTH), (t, Q_WIDTH), BF16),
            ("qbrot", row(Q_WIDTH), (t, Q_WIDTH), BF16),
            ("kvc", pl.BlockSpec((2, tm, KV_WIDTH), lambda i: (0, i, 0)), (2, t, KV_WIDTH), F32),
            ("ks", row(KV_DUP), (t, KV_DUP), BF16),
            ("vst", seq_on_lanes(KV_HEADS * SEL_V_ROWS), (b, KV_HEADS * SEL_V_ROWS, s), BF16),
            ("kw", row(KV_DUP), (t, KV_DUP), BF16),
            ("vwt", seq_on_lanes(KV_WIDTH), (b, KV_WIDTH, s), BF16),
            ("gnt", seq_on_lanes(KV_HEADS * GATE_ROWS), (b, KV_HEADS * GATE_ROWS, s), F32),
            ("ga", row(D_MODEL), (t, D_MODEL), BF16),
            ("gb", row(D_MODEL), (t, D_MODEL), BF16)]
    res = pl.pallas_call(
        _inproj_kernel,
        grid=(t // tm,),
        in_specs=[row(D_MODEL), fixed(1, D_MODEL), fixed(D_MODEL, _IN_COLS_BIG), row(LANES), row(LANES)],
        out_specs=[spec for _, spec, _, _ in outs],
        out_shape=[jax.ShapeDtypeStruct(shape, dt) for _, _, shape, dt in outs],
        compiler_params=_params(("parallel",)),
        name="in_proj",
    )(x2, gain.reshape(1, D_MODEL), w_big, cos, sin)
    return {name: o for (name, _, _, _), o in zip(outs, res)}


def _compress_kernel(flat_ref, pe_ref, w1t_ref, w1b_ref, b1_ref, w2_ref, o_ref):
    f = flat_ref[...]
    pe = pe_ref[...]
    top = jnp.dot((f + pe[0:1]).astype(BF16), w1t_ref[...], preferred_element_type=F32)
    bot = jnp.dot((f + pe[1:2]).astype(BF16), w1b_ref[...], preferred_element_type=F32)
    ncp = f.shape[0]
    hid = _gelu_tanh(top + pltpu.roll(bot, ncp - 1, axis=0) + b1_ref[...])
    o_ref[...] = jnp.dot(hid.astype(BF16), w2_ref[...], preferred_element_type=F32).astype(BF16)


def _compress_weights(pos_emb, w1, b1, w2):
    half = CMP_LEN // 2
    eye = jnp.eye(KV_HEADS, dtype=F32)
    w1r = w1.reshape(2, CMP_LEN, HEAD_DIM, CMP_HIDDEN)

    def big(part):
        return jnp.einsum("wtdh,gk->wtgdkh", part, eye).reshape(
            2, half * KV_WIDTH, KV_HEADS * CMP_HIDDEN).astype(BF16)

    def pe_big(part):
        return jnp.broadcast_to(part[:, :, None, :], (2, half, KV_HEADS, HEAD_DIM)).reshape(2, half * KV_WIDTH)

    pe = jnp.stack([pe_big(pos_emb[:, :half]), pe_big(pos_emb[:, half:])], axis=1)
    b1b = jnp.tile(b1, (1, KV_HEADS)).reshape(2, 1, KV_HEADS * CMP_HIDDEN)
    w2b = jnp.einsum("whd,gk,r->wghkrd", w2, eye, jnp.ones((2,), F32)).reshape(
        2, KV_HEADS * CMP_HIDDEN, KV_DUP).astype(BF16)
    return pe, big(w1r[:, :half]), big(w1r[:, half:]), b1b, w2b


def _compress(flat, cw):
    pe, w1t, w1b, b1b, w2b = cw
    _, b, ncp, fw = flat.shape
    hw = KV_HEADS * CMP_HIDDEN
    per_w = lambda *shape: pl.BlockSpec((None,) + shape, lambda w, bb: (w,) + (0,) * len(shape))
    return pl.pallas_call(
        _compress_kernel,
        grid=(2, b),
        in_specs=[pl.BlockSpec((None, None, ncp, fw), lambda w, bb: (w, bb, 0, 0)),
                  per_w(2, fw), per_w(fw, hw), per_w(fw, hw), per_w(1, hw), per_w(hw, KV_DUP)],
        out_specs=pl.BlockSpec((None, None, ncp, KV_DUP), lambda w, bb: (w, bb, 0, 0)),
        out_shape=jax.ShapeDtypeStruct((2, b, ncp, KV_DUP), BF16),
        compiler_params=_params(("parallel", "parallel")),
        name="compress",
    )(flat, pe, w1t, w1b, b1b, w2b)


def _stack_heads(q):
    lane = lax.broadcasted_iota(jnp.int32, (BLOCK, LANES), 1)
    lo = lane < HEAD_DIM
    zero = jnp.zeros((BLOCK, LANES), q.dtype)
    p0, p1 = q[:, :LANES], q[:, LANES:]
    return jnp.concatenate([jnp.where(lo, p0, zero), jnp.where(lo, zero, p0),
                            jnp.where(lo, p1, zero), jnp.where(lo, zero, p1)], axis=0)


def _unstack_out(o_t):
    lane = lax.broadcasted_iota(jnp.int32, (BLOCK, LANES), 1)
    lo = lane < HEAD_DIM
    o2 = jnp.concatenate([o_t, o_t], axis=0)
    t = [o2[:, h * BLOCK:(h + 1) * BLOCK].T for h in range(GROUP)]
    return jnp.concatenate([jnp.where(lo, t[0], t[1]), jnp.where(lo, t[2], t[3])], axis=1)


def _gate_row(g, branch):
    return jnp.concatenate([g[branch * GROUP + h:branch * GROUP + h + 1, :] for h in range(GROUP)], axis=1)


def _nt_dot(a, b):
    return lax.dot_general(a, b, (((1,), (1,)), ((), ())), preferred_element_type=F32)


def _attn_specs(s):
    nb = s // BLOCK
    q_spec = pl.BlockSpec((None, BLOCK, GROUP * HEAD_DIM), lambda b, g, i: (b, i, g))
    k_spec = pl.BlockSpec((None, s, LANES), lambda b, g, i: (b, 0, g))
    vt_spec = pl.BlockSpec((None, HEAD_DIM, s), lambda b, g, i: (b, g, 0))
    gate_spec = pl.BlockSpec((None, None, GATE_ROWS,BLOCK), lambda b, g, i: (b, g, 0, i))
    return nb, q_spec, k_spec, vt_spec, gate_spec


def _band_kernel(*refs, window, nkb, use_sink, gate_branch, has_prev):
    refs = list(refs)
    q_ref, k_ref, vt_ref = refs[:3]
    pos = 3
    g_ref = sink_ref = prev_ref = None
    if gate_branch is not None:
        g_ref = refs[pos]; pos += 1
    if use_sink:
        sink_ref = refs[pos]; pos += 1
    if has_prev:
        prev_ref = refs[pos]; pos += 1
    o_ref = refs[pos]

    length = nkb * BLOCK
    for sub in range(BAND_QBLOCKS):
        i = pl.program_id(2) * BAND_QBLOCKS + sub
        rows = slice(sub * BLOCK, (sub + 1) * BLOCK)
        start = pl.multiple_of(jnp.maximum(i - (nkb - 1), 0) * BLOCK, BLOCK)
        kb = k_ref[pl.ds(start, length), :]
        vb = vt_ref[:, pl.ds(start, length)]
        qs = _stack_heads(q_ref[rows, :])
        kpos = start + lax.broadcasted_iota(jnp.int32, (length, BLOCK), 0)
        qpos = i * BLOCK + lax.broadcasted_iota(jnp.int32, (length, BLOCK), 1)
        rel = qpos - kpos
        bias = jnp.where((rel >= 0) & (rel < window), 0.0, NEG_INF).astype(F32)
        s = _nt_dot(kb, qs) + jnp.concatenate([bias] * GROUP, axis=1)
        m = jnp.max(s, axis=0, keepdims=True)
        if use_sink:
            sink = sink_ref[...]
            m = jnp.maximum(m, sink)
        p = jnp.exp2(s - m)
        l = jnp.sum(p, axis=0, keepdims=True)
        if use_sink:
            l = l + jnp.exp2(sink - m)
        o_t = jnp.dot(vb, p.astype(BF16), preferred_element_type=F32)
        scale = 1.0 / jnp.maximum(l, TINY)
        if gate_branch is not None:
            scale = scale * _gate_row(g_ref[:, rows], gate_branch)
        out = _unstack_out(o_t * scale)
        if has_prev:
            out = out + prev_ref[rows, :]
        o_ref[rows, :] = out.astype(o_ref.dtype)


def _band_attention(q, k, vt, *, window, gates=None, gate_branch=None, sinks=None, prev=None, out_dtype=F32):
    b, s, _ = q.shape
    nb, _, k_spec, vt_spec, _ = _attn_specs(s)
    rows = BAND_QBLOCKS * BLOCK
    q_spec = pl.BlockSpec((None, rows, GROUP * HEAD_DIM), lambda b_, g, i: (b_, i, g))
    gate_spec = pl.BlockSpec((None, None, GATE_ROWS,rows), lambda b_, g, i: (b_, g, 0, i))
    nkb = -(-window // BLOCK) + 1
    ins, specs = [q, k, vt], [q_spec, k_spec, vt_spec]
    if gate_branch is not None:
        ins.append(gates); specs.append(gate_spec)
    if sinks is not None:
        ins.append(sinks)
        specs.append(pl.BlockSpec((None, 1, GROUP * BLOCK), lambda b_, g, i: (g, 0, 0)))
    if prev is not None:
        ins.append(prev); specs.append(q_spec)
    kern = functools.partial(_band_kernel, window=window, nkb=nkb, use_sink=sinks is not None,
                             gate_branch=gate_branch, has_prev=prev is not None)
    return pl.pallas_call(
        kern,
        grid=(b, KV_HEADS, nb // BAND_QBLOCKS),
        in_specs=specs,
        out_specs=q_spec,
        out_shape=jax.ShapeDtypeStruct((b, s, Q_WIDTH), out_dtype),
        compiler_params=_params(("parallel", "parallel", "parallel")),
        name=f"band_attn_w{window}",
    )(*ins)


def _cmp_kernel(q_ref, kc_ref, vct_ref, g_ref, mt_ref, o_ref, sel_ref, *, n_sel):
    step = pl.program_id(2)
    steps = pl.num_programs(2)
    ncp_all, ns_all = kc_ref.shape[0], mt_ref.shape[0]

    def prefix(ncp, ns):
        for sub in range(CMP_QBLOCKS):
            _cmp_block(step * CMP_QBLOCKS + sub, slice(sub * BLOCK, (sub + 1) * BLOCK),
                       q_ref, kc_ref, vct_ref, g_ref, mt_ref, o_ref, sel_ref, n_sel=n_sel, ncp=ncp, ns=ns)

    for k in range(CMP_PREFIXES):
        ncp = ncp_all * (k + 1) // CMP_PREFIXES
        ns = min(ns_all, -(-(ns_all * (k + 1) // CMP_PREFIXES) // LANES) * LANES)
        pl.when(step * CMP_PREFIXES // steps == k)(functools.partial(prefix, ncp, ns))


def _cmp_block(i, rows, q_ref, kc_ref, vct_ref, g_ref, mt_ref, o_ref, sel_ref, *, n_sel, ncp, ns):
    qs = _stack_heads(q_ref[rows, :])
    c = lax.broadcasted_iota(jnp.int32, (ncp, BLOCK), 0)
    t = i * BLOCK + lax.broadcasted_iota(jnp.int32, (1, BLOCK), 1)
    last = (t - (CMP_LEN - 1)) >> (CMP_STRIDE.bit_length() - 1)
    bias = jnp.where(c <= last, 0.0, NEG_INF).astype(F32)
    s = _nt_dot(kc_ref[0:ncp, :], qs) + jnp.concatenate([bias] * GROUP, axis=1)
    m = jnp.max(s, axis=0, keepdims=True)
    m = jnp.where(m == NEG_INF, 0.0, m)
    p = jnp.exp2(s - m)
    l = jnp.sum(p, axis=0, keepdims=True)
    p = p * (1.0 / jnp.maximum(l, TINY))
    o_t = jnp.dot(vct_ref[:, 0:ncp], p.astype(BF16), preferred_element_type=F32)
    o_ref[rows, :] = _unstack_out(o_t * _gate_row(g_ref[:, rows], 0))

    imp = p[:, 0:BLOCK]
    for h in range(1, GROUP):
        imp = imp + p[:, h * BLOCK:(h + 1) * BLOCK]
    hi = imp.astype(BF16)
    r1 = imp - hi.astype(F32)
    mid = r1.astype(BF16)
    lo = (r1 - mid.astype(F32)).astype(BF16)
    mt = mt_ref[0:ns, 0:ncp]
    imp_s = (jnp.dot(mt, hi, preferred_element_type=F32) + jnp.dot(mt, mid, preferred_element_type=F32)
             + jnp.dot(mt, lo, preferred_element_type=F32))

    j =lax.broadcasted_iota(jnp.int32, imp_s.shape, 0)
    tq = i * BLOCK + lax.broadcasted_iota(jnp.int32, imp_s.shape, 1)
    cur = tq >> (SLC_LEN.bit_length() - 1)
    valid = j <= cur
    forced = (j == 0) | (valid & (j > cur - N_LOCAL_SLC))
    work = jnp.where(valid & jnp.logical_not(forced), imp_s, NEG_INF)
    sel_bias = jnp.where(forced, 0.0, NEG_BIG).astype(F32)
    for _ in range(n_sel - 1 - N_LOCAL_SLC):
        mx = jnp.max(work, axis=0, keepdims=True)
        first = jnp.min(jnp.where(work == mx, j, ns), axis=0, keepdims=True)
        pick = j == first
        sel_bias = jnp.where(pick, 0.0, sel_bias)
        work = jnp.where(pick, NEG_INF, work)
    sel_bias = jnp.where(valid, sel_bias, NEG_BIG)
    sel_q = sel_bias.T.astype(BF16)
    pad = sel_ref.shape[1] - ns
    sel_ref[rows, :] = jnp.concatenate([sel_q, jnp.full((BLOCK, pad), NEG_BIG, BF16)], axis=1)


def _overlap_map(ns, ncp):
    ratio_c = CMP_LEN // CMP_STRIDE
    ratio_s = SLC_LEN // CMP_STRIDE
    m = np.zeros((ns, ncp), np.float32)
    for jj in range(ns):
        for a in range(ratio_s):
            for n in range(ratio_c):
                c = ratio_s * jj + a - n
                if 0 <= c < ncp - ratio_c + 1:
                    m[jj, c] += 1.0
    return jnp.asarray(m, BF16)


def _cmp_attention(q_raw, kc, vct, gates):
    b, s, _ = q_raw.shape
    nb = s // BLOCK
    rows = CMP_QBLOCKS * BLOCK
    assert nb % (CMP_QBLOCKS * CMP_PREFIXES) == 0
    q_spec = pl.BlockSpec((None, rows, GROUP * HEAD_DIM), lambda b_, g, i: (b_, i, g))
    gate_spec = pl.BlockSpec((None, None, GATE_ROWS,rows), lambda b_, g, i: (b_, g, 0, i))
    ncp = s // CMP_STRIDE
    ns = s // SLC_LEN
    assert ns >= N_SLC
    nsp = -(-ns // LANES) * LANES + LANES
    mt = _overlap_map(ns, ncp)
    kern = functools.partial(_cmp_kernel, n_sel=N_SLC)
    return pl.pallas_call(
        kern,
        grid=(b, KV_HEADS, nb // CMP_QBLOCKS),
        in_specs=[q_spec,
                  pl.BlockSpec((None, ncp, LANES), lambda b_, g, i: (b_, 0, g)),
                  pl.BlockSpec((None, HEAD_DIM, ncp), lambda b_, g, i: (b_, 2 * g, 0)),
                  gate_spec,
                  pl.BlockSpec((ns, ncp), lambda b_, g, i: (0, 0))],
        out_specs=[q_spec, pl.BlockSpec((None, None, rows, nsp), lambda b_, g, i: (b_, g, i, 0))],
        out_shape=[jax.ShapeDtypeStruct((b, s, Q_WIDTH), F32),
                   jax.ShapeDtypeStruct((b, KV_HEADS, s, nsp), BF16)],
        compiler_params=_params(("parallel", "parallel", "parallel")),
        name="cmp_attn_topk",
    )(q_raw, kc, vct, gates, mt)


CMP_PREFIXES = 4
CMP_QBLOCKS = 4
SEL_BLOCKS_PER_TILE = SEL_TILE // SLC_LEN
SEL_TILES_PER_LANES = LANES // SEL_BLOCKS_PER_TILE
SEL_DIAG_VARIANTS = 4
SEL_V_ROWS = HEAD_DIM + 16


def _sel_kernel(q_ref, k_ref, vt_ref, sel_ref, oh_ref, g_ref, prev_ref, o_ref,
                s0_sc, sa_sc, sb_sc, mxa_sc, mxb_sc, ra_sc, rb_sc, m_sc, acc_sc):
    i = pl.program_id(2)
    tk = SEL_TILE
    qs = _stack_heads(q_ref[...])
    diag = i // (tk // BLOCK)
    masked_lanes = sel_ref.shape[1] - LANES
    bufs = ((sa_sc, mxa_sc, ra_sc), (sb_sc, mxb_sc, rb_sc))

    def scores(tile, lane0):
        k0 = pl.multiple_of(tile * tk, tk)
        feat = sel_ref[:, pl.ds(pl.multiple_of(lane0, LANES), LANES)]
        qa = jnp.concatenate([qs, jnp.concatenate([feat] * GROUP, axis=0)], axis=1)
        ka = jnp.concatenate([k_ref[pl.ds(k0, tk), :], oh_ref[tile % SEL_TILES_PER_LANES]], axis=1)
        return _nt_dot(ka, qa)

    def values(tile):
        return vt_ref[:, pl.ds(pl.multiple_of(tile * tk, tk), tk)]

    def tile_or_pad(tile):
        return jnp.minimum(tile, jnp.maximum(diag - 1, 0))

    def produce(tile, buf):
        tc = tile_or_pad(tile)
        lane0 = jnp.where(tile < diag, (tc // SEL_TILES_PER_LANES) * LANES, masked_lanes)
        sn = scores(tc, lane0)
        ref = m_sc[...]
        buf[0][...] = (sn - ref).astype(BF16)
        buf[1][...] = jnp.max(sn, axis=0, keepdims=True)
        buf[2][...] = ref

    def consume(tile, buf):
        m_old = m_sc[...]
        m_new = jnp.maximum(m_old, buf[1][...])
        alpha = jnp.exp2(m_old - m_new)
        shift = (buf[2][...] - m_new).astype(BF16)
        p = jnp.exp2(buf[0][...] + shift)
        acc_sc[...] = alpha * acc_sc[...] + jnp.dot(values(tile_or_pad(tile)), p, preferred_element_type=F32)
        m_sc[...] = m_new

    def diagonal(rows):
        k0 = pl.multiple_of(diag * tk, tk)
        lane0 = pl.multiple_of((diag // SEL_TILES_PER_LANES) * LANES, LANES)
        qa = jnp.concatenate([qs, jnp.concatenate([sel_ref[:, pl.ds(lane0, LANES)]] * GROUP, axis=0)], axis=1)
        ka = jnp.concatenate([k_ref[pl.ds(k0, rows), :], oh_ref[diag % SEL_TILES_PER_LANES, 0:rows, :]], axis=1)
        kpos = k0 + lax.broadcasted_iota(jnp.int32, (rows, BLOCK), 0)
        qpos = i * BLOCK + lax.broadcasted_iota(jnp.int32, (rows, BLOCK), 1)
        causal = jnp.where(kpos <= qpos, 0.0, NEG_BIG).astype(F32)
        s0 = _nt_dot(ka, qa) + jnp.concatenate([causal] * GROUP, axis=1)
        m0 = jnp.max(s0, axis=0, keepdims=True)
        m_sc[...] = m0
        s0_sc[0:rows, :] = s0
        produce(0, bufs[0])
        p0 = jnp.exp2(s0_sc[0:rows, :] - m0).astype(BF16)
        acc_sc[...] = jnp.dot(vt_ref[:, pl.ds(k0, rows)], p0, preferred_element_type=F32)

    blocks_per_tile = tk // BLOCK
    for v in range(SEL_DIAG_VARIANTS):
        rows = tk * (v + 1) // SEL_DIAG_VARIANTS
        pl.when((i % blocks_per_tile) * SEL_DIAG_VARIANTS // blocks_per_tile == v)(
            functools.partial(diagonal, rows))

    def sweep(n_tiles):
        for tile in range(n_tiles):
            if tile + 1 < n_tiles:
                produce(tile + 1, bufs[(tile + 1) % 2])
            consume(tile, bufs[tile % 2])

    n_pairs = (diag + 1) // 2
    for n in range(1, k_ref.shape[0] // tk // 2 + 1):
        pl.when(n_pairs == n)(functools.partial(sweep, 2 * n))
    acc = acc_sc[...]
    scale = _gate_row(g_ref[...], 1) / jnp.maximum(acc[HEAD_DIM:HEAD_DIM + 1, :], TINY)
    o_ref[...] = _unstack_out(acc[0:HEAD_DIM, :] * scale) + prev_ref[...]


def _block_onehots():
    jm = np.arange(SEL_TILES_PER_LANES)[:, None, None]
    r = np.arange(SEL_TILE)[None, :, None]
    c = np.arange(LANES)[None, None, :]
    return jnp.asarray(c == jm * SEL_BLOCKS_PER_TILE + r // SLC_LEN, BF16)


def _sel_attention(q_rot, ks, vst, sel, gates, prev):
    b, s, _ = q_rot.shape
    nb, q_spec, k_spec, _, gate_spec = _attn_specs(s)
    vt_spec = pl.BlockSpec((None, SEL_V_ROWS, s), lambda b_, g, i: (b_, g, 0))
    nsp = sel.shape[-1]
    cols = GROUP * BLOCK
    return pl.pallas_call(
        _sel_kernel,
        grid=(b, KV_HEADS, nb),
        in_specs=[q_spec, k_spec, vt_spec,
                  pl.BlockSpec((None, None, BLOCK, nsp), lambda b_, g, i: (b_, g, i, 0)),
                  pl.BlockSpec((SEL_TILES_PER_LANES, SEL_TILE, LANES), lambda b_, g, i: (0, 0, 0)),
                  gate_spec, q_spec],
        out_specs=q_spec,
        out_shape=jax.ShapeDtypeStruct((b, s, Q_WIDTH), F32),
        scratch_shapes=[pltpu.VMEM((SEL_TILE, cols), F32),
                        pltpu.VMEM((SEL_TILE, cols), BF16), pltpu.VMEM((SEL_TILE, cols), BF16),
                        pltpu.VMEM((1, cols), F32), pltpu.VMEM((1, cols), F32),
                        pltpu.VMEM((1, cols), F32), pltpu.VMEM((1, cols), F32),
                        pltpu.VMEM((1, cols), F32),
                        pltpu.VMEM((SEL_V_ROWS, cols), F32)],
        compiler_params=_params(("parallel", "parallel", "parallel")),
        name="sel_attn",
    )(q_rot, ks, vst, sel, _block_onehots(), gates, prev)


def _merge_kernel(oa_ref, ob_ref, ga_ref, gb_ref, x_ref, wa_ref, wb_ref, wo_ref, gain_ref, o_ref):
    ya = jnp.dot(oa_ref[...], wa_ref[...], preferred_element_type=F32)
    yb = jnp.dot(ob_ref[...].astype(BF16), wb_ref[...], preferred_element_type=F32)
    mixed = ga_ref[...].astype(F32) * ya + gb_ref[...].astype(F32) * yb
    z = jnp.dot(mixed.astype(BF16), wo_ref[...], preferred_element_type=F32)
    o_ref[...] = x_ref[...] + _rms(z, gain_ref[...])


def _merge(o_a, o_b, ga, gb, x2, w_a, w_b, w_o, gain):
    t = x2.shape[0]
    tm = 512
    row = lambda w: pl.BlockSpec((tm, w), lambda i: (i, 0))
    fixed = lambda r, c: pl.BlockSpec((r, c), lambda i: (0, 0))
    return pl.pallas_call(
        _merge_kernel,
        grid=(t // tm,),
        in_specs=[row(Q_WIDTH), row(Q_WIDTH), row(D_MODEL), row(D_MODEL), row(D_MODEL),
                  fixed(Q_WIDTH, D_MODEL), fixed(Q_WIDTH, D_MODEL), fixed(D_MODEL, D_MODEL), fixed(1, D_MODEL)],
        out_specs=row(D_MODEL),
        out_shape=jax.ShapeDtypeStruct((t, D_MODEL), F32),
        compiler_params=_params(("parallel",)),
        name="merge_out_proj",
    )(o_a, o_b, ga, gb, x2, w_a.astype(BF16), w_b.astype(BF16), w_o.astype(BF16), gain.reshape(1, D_MODEL))


def _ffn_kernel(xp_ref, x_ref, gpre_ref, wu_ref, cw_ref, cb_ref, wd_ref, gpost_ref, o_ref,
                h_sc, ua_sc, ub_sc, g_sc, *, tm, tiles_per_seq, nck):
    i = pl.program_id(0)
    gpre = gpre_ref[...]
    keep = (i % tiles_per_seq != 0).astype(F32)
    h_sc[0:FFN_HALO, :] = (_rms(xp_ref[...], gpre) * keep).astype(BF16)
    h_sc[FFN_HALO:, :] = _rms(x_ref[...], gpre).astype(BF16)

    def produce(n, u_sc):
        h = h_sc[...]
        u_sc[0] = jnp.dot(h, wu_ref[n], preferred_element_type=F32)
        u_sc[1] = jnp.dot(h, wu_ref[nck + n], preferred_element_type=F32)

    def consume(n, u_sc):
        def conv(part, col):
            cw = cw_ref[col]
            return (cb_ref[col] + cw[0:1] * u_sc[part, pl.ds(FFN_HALO - 2, tm), :]
                    + cw[1:2] * u_sc[part, pl.ds(FFN_HALO - 1, tm), :]
                    + cw[2:3] * u_sc[part, pl.ds(FFN_HALO, tm), :])

        g_sc[n] = (_gelu_tanh(conv(0, n)) * conv(1, nck + n)).astype(BF16)

    bufs = (ua_sc, ub_sc)
    produce(0, bufs[0])
    for n in range(nck):
        if n + 1 < nck:
            produce(n + 1, bufs[(n + 1) % 2])
        consume(n, bufs[n % 2])
    g = jnp.concatenate([g_sc[n] for n in range(nck)], axis=1)
    z = jnp.dot(g, wd_ref[...], preferred_element_type=F32)
    o_ref[...] = x_ref[...] + _rms(z, gpost_ref[...])


def _ffn(x2, seq, gpre, w_up, conv_w, conv_b, w_down, gpost):
    t = x2.shape[0]
    tm = min(512, seq)
    ck = FFN_CHUNK
    nck = D_FF // ck
    assert nck % 2 == 1
    kern = functools.partial(_ffn_kernel, tm=tm, tiles_per_seq=seq // tm, nck=nck)
    halo_blocks = tm // FFN_HALO
    chunks = lambda a: jnp.transpose(a.reshape(a.shape[0], 2 * nck, ck), (1, 0, 2))
    whole = lambda *shape: pl.BlockSpec(shape, lambda i: (0,) * len(shape), pipeline_mode=pl.Buffered(1))
    return pl.pallas_call(
        kern,
        grid=(t // tm,),
        in_specs=[pl.BlockSpec((FFN_HALO, D_MODEL), lambda i: (jnp.maximum(i * halo_blocks - 1, 0), 0)),
                  pl.BlockSpec((tm, D_MODEL), lambda i: (i, 0)),
                  whole(1, D_MODEL),
                  whole(2 * nck, D_MODEL, ck),
                  whole(2 * nck, 3, ck),
                  whole(2 * nck, 1, ck),
                  whole(D_FF, D_MODEL),
                  whole(1, D_MODEL)],
        out_specs=pl.BlockSpec((tm, D_MODEL), lambda i: (i, 0)),
        out_shape=jax.ShapeDtypeStruct((t, D_MODEL), F32),
        scratch_shapes=[pltpu.VMEM((tm + FFN_HALO, D_MODEL), BF16),
                        pltpu.VMEM((2, tm + FFN_HALO, ck), F32),
                        pltpu.VMEM((2, tm + FFN_HALO, ck), F32),
                        pltpu.VMEM((nck, tm, ck), BF16)],
        compiler_params=_params(("parallel",)),
        name="conv_ffn",
    )(x2, x2, gpre.reshape(1, D_MODEL), chunks(w_up.astype(BF16)), chunks(conv_w),
      chunks(conv_b.reshape(1, 2 * D_FF)), w_down.astype(BF16), gpost.reshape(1, D_MODEL))


def _layer(x2, b, s, cos, sin, p):
    t = b * s
    u = _inproj(x2, b, s, p["attn_pre_gain"], _inproj_weight(p["w_in"]), cos, sin)
    r3 = lambda a: a.reshape(b, s, a.shape[-1])

    gates = u["gnt"].reshape(b, KV_HEADS, GATE_ROWS, s)
    sinks = jnp.repeat(p["attn_sinks"].astype(F32) * LOG2E, BLOCK).reshape(KV_HEADS, 1, GROUP * BLOCK)

    o_a = _band_attention(r3(u["qa"]), r3(u["ka"]), u["vat"], window=A_WINDOW, sinks=sinks, out_dtype=BF16)

    ncp = s // CMP_STRIDE
    flat = u["kvc"].reshape(2, b, ncp, CMP_STRIDE * KV_WIDTH)
    cmp = _compress(flat, _compress_weights(p["cmp_pos_emb"], p["cmp_w1"], p["cmp_b1"], p["cmp_w2"]))
    o_c, sel = _cmp_attention(r3(u["qbraw"]), cmp[0], jnp.transpose(cmp[1], (0, 2, 1)), gates)
    o_cs = _sel_attention(r3(u["qbrot"]), r3(u["ks"]), u["vst"], sel, gates, o_c)
    o_b = _band_attention(r3(u["qbrot"]), r3(u["kw"]), u["vwt"], window=B_WINDOW,
                          gates=gates, gate_branch=2, prev=o_cs)

    x2 = _merge(o_a.reshape(t, Q_WIDTH), o_b.reshape(t, Q_WIDTH), u["ga"], u["gb"], x2,
                p["w_branch_a"], p["w_branch_b"], p["w_out"], p["attn_post_gain"])
    return _ffn(x2, s, p["ffn_pre_gain"], p["ffn_w_up"], p["ffn_conv_w"], p["ffn_conv_b"], p["ffn_w_down"],
                p["ffn_post_gain"])


def kernel(x, positions, attn_pre_gain, attn_post_gain, ffn_pre_gain, ffn_post_gain, w_in, attn_sinks,
           cmp_pos_emb, cmp_w1, cmp_b1, cmp_w2, w_branch_a, w_branch_b, w_out, ffn_w_up, ffn_conv_w,
           ffn_conv_b, ffn_w_down):
    b, s, d = x.shape
    params = dict(attn_pre_gain=attn_pre_gain, attn_post_gain=attn_post_gain, ffn_pre_gain=ffn_pre_gain,
                  ffn_post_gain=ffn_post_gain, w_in=w_in, attn_sinks=attn_sinks, cmp_pos_emb=cmp_pos_emb,
                  cmp_w1=cmp_w1, cmp_b1=cmp_b1, cmp_w2=cmp_w2, w_branch_a=w_branch_a, w_branch_b=w_branch_b,
                  w_out=w_out, ffn_w_up=ffn_w_up, ffn_conv_w=ffn_conv_w, ffn_conv_b=ffn_conv_b,
                  ffn_w_down=ffn_w_down)
    cos, sin = _rope_tables(positions)
    x2 = x.reshape(b * s, d)
    for layer in range(attn_pre_gain.shape[0]):
        x2 = _layer(x2, b, s, cos, sin, {k: v[layer] for k, v in params.items()})
    return x2.reshape(b, s, d)
```

```python
import functools

import numpy as np
import jax
import jax.numpy as jnp
from jax import lax
from jax.experimental import pallas as pl
from jax.experimental.pallas import tpu as pltpu

F32 = jnp.float32
BF16 = jnp.bfloat16

D_MODEL = 1024
HEAD_DIM = 64
BLOCK = 128
Q_HEADS = 8
KV_HEADS = 2
GROUP = Q_HEADS // KV_HEADS
A_WINDOW = 128
B_WINDOW = 512
CMP_LEN = 32
CMP_STRIDE = 16
CMP_HIDDEN = 128
SLC_LEN = 64
N_SLC = 16
N_LOCAL_SLC = 2
D_FF = 2816
ROPE_THETA = 10000.0
EPS = 1e-6
TINY = 1e-30
NEG_INF = float("-inf")
LOG2E = 1.4426950408889634
Q_SCALE = HEAD_DIM ** -0.5 * LOG2E
NEG_BIG = -1e30

LANES = 128
VMEM_LIMIT = 56 * 1024 * 1024

Q_WIDTH = Q_HEADS * HEAD_DIM
KV_WIDTH = KV_HEADS * HEAD_DIM
KV_DUP = 2 * KV_WIDTH
N_GATES = 3 * Q_HEADS
GATE_ROWS = 16

SEL_TILE = 1024
BAND_QBLOCKS = 8
FFN_CHUNK = 256
FFN_HALO = 16


def _params(sem, flags=None):
    return pltpu.CompilerParams(dimension_semantics=sem, vmem_limit_bytes=VMEM_LIMIT, flags=flags)


def _rms(x, gain):
    return x * lax.rsqrt(jnp.mean(x * x, axis=-1, keepdims=True) + EPS) * gain


def _gelu_tanh(x):
    return 0.5 * x * (1.0 + jnp.tanh(0.7978845608028654 * (x + 0.044715 * (x * x * x))))


def _sigmoid(x):
    return 1.0 / (1.0 + jnp.exp(-x))


def _rope_table_kernel(pos_ref, invf_ref, cos_ref, sin_ref):
    ang = pos_ref[...].astype(F32) * invf_ref[...]
    lane = lax.broadcasted_iota(jnp.int32, ang.shape, 1)
    sign = jnp.where((lane & (HEAD_DIM - 1)) < HEAD_DIM // 2, -1.0, 1.0).astype(F32)
    cos_ref[...] = jnp.cos(ang)
    sin_ref[...] = jnp.sin(ang) * sign


def _rope_tables(positions):
    t = positions.size
    tm = min(1024, t)
    half = HEAD_DIM // 2
    inv_freq = ROPE_THETA ** (-jnp.arange(half, dtype=F32) / half)
    invf = jnp.tile(inv_freq, LANES // half).reshape(1, LANES)
    return pl.pallas_call(
        _rope_table_kernel,
        grid=(t // tm,),
        in_specs=[pl.BlockSpec((tm, 1), lambda i: (i, 0)),
                  pl.BlockSpec((1, LANES), lambda i: (0, 0))],
        out_specs=[pl.BlockSpec((tm, LANES), lambda i: (i, 0))] * 2,
        out_shape=[jax.ShapeDtypeStruct((t, LANES), F32)] * 2,
        compiler_params=_params(("parallel",)),
        name="rope_tables",
    )(positions.reshape(t, 1), invf)


def _rope(u, cos, sin):
    lane = lax.broadcasted_iota(jnp.int32, cos.shape, 1)
    first_half = (lane & (HEAD_DIM - 1)) < HEAD_DIM // 2
    outs = []
    for c in range(u.shape[1] // LANES):
        x = u[:, c * LANES:(c + 1) * LANES]
        partner = jnp.where(first_half,
                            pltpu.roll(x, LANES - HEAD_DIM // 2, axis=1),
                            pltpu.roll(x, HEAD_DIM // 2, axis=1))
        outs.append(x * cos + partner * sin)
    return outs[0] if len(outs) == 1 else jnp.concatenate(outs, axis=1)


_IN_SEGS = (("qa", Q_WIDTH), ("ka", KV_DUP), ("va", KV_WIDTH), ("qb", Q_WIDTH),
            ("kc", KV_WIDTH), ("vc", KV_WIDTH), ("ks", KV_DUP), ("vs", KV_WIDTH),
            ("kw", KV_DUP), ("vw", KV_WIDTH), ("gn", LANES), ("ga", D_MODEL), ("gb", D_MODEL))
_IN_COLS_BIG = sum(w for _, w in _IN_SEGS)


def _inproj_kernel(x_ref, gain_ref, w_ref, cos_ref, sin_ref,
                   qa_ref, ka_ref, vat_ref, qbraw_ref, qbrot_ref, kvc_ref,
                   ks_ref, vst_ref, kw_ref, vwt_ref, gnt_ref, ga_ref, gb_ref):
    h = _rms(x_ref[...], gain_ref[...]).astype(BF16)
    cos = cos_ref[...]
    sin = sin_ref[...]
    offs = {}
    off = 0
    for name, width in _IN_SEGS:
        offs[name] = (off, width)
        off += width

    def proj(name):
        o, w = offs[name]
        return jnp.dot(h, w_ref[:, o:o + w], preferred_element_type=F32)

    qa_ref[...] = (_rope(proj("qa"), cos, sin) * Q_SCALE).astype(BF16)
    ka_ref[...] = _rope(proj("ka"), cos, sin).astype(BF16)
    vat_ref[...] = proj("va").T.astype(BF16)
    qb = proj("qb")
    qbraw_ref[...] = (qb * Q_SCALE).astype(BF16)
    qbrot_ref[...] = (_rope(qb, cos, sin) * Q_SCALE).astype(BF16)
    kvc_ref[0] = proj("kc")
    kvc_ref[1] = proj("vc")
    ks_ref[...] = _rope(proj("ks"), cos, sin).astype(BF16)
    vs_t = proj("vs").T
    tm = vs_t.shape[1]
    ones_row = (lax.broadcasted_iota(jnp.int32, (SEL_V_ROWS - HEAD_DIM, tm), 0) == 0).astype(F32)
    vst_ref[...] = jnp.concatenate(
        [piece for g in range(KV_HEADS) for piece in (vs_t[g * HEAD_DIM:(g + 1) * HEAD_DIM], ones_row)],
        axis=0).astype(BF16)
    kw_ref[...] = _rope(proj("kw"), cos, sin).astype(BF16)
    vwt_ref[...] = proj("vw").T.astype(BF16)
    gnt_ref[...] = _sigmoid(proj("gn")).T[0:KV_HEADS * GATE_ROWS, :]
    ga_ref[...] = _sigmoid(proj("ga")).astype(BF16)
    gb_ref[...] = _sigmoid(proj("gb")).astype(BF16)


def _dup_heads(w):
    g0, g1 = w[:, :HEAD_DIM], w[:, HEAD_DIM:]
    return jnp.concatenate([g0, g0, g1, g1], axis=1)


def _inproj_weight(w_in):
    sizes = (Q_WIDTH, KV_WIDTH, KV_WIDTH, Q_WIDTH) + (KV_WIDTH,) * 6 + (N_GATES, D_MODEL, D_MODEL)
    offs = np.cumsum(sizes)[:-1].tolist()
    qa, ka, va, qb, kc, vc, ks, vs, kw, vw, gn, ga, gb = jnp.split(w_in, offs, axis=1)
    order = [(g * GROUP + h) * 3 + c for g in range(KV_HEADS) for c in range(3) for h in range(GROUP)]
    gn = gn[:, np.asarray(order)].reshape(D_MODEL, KV_HEADS, 3 * GROUP)
    gn = jnp.pad(gn, ((0, 0), (0, 0), (0, GATE_ROWS - 3 * GROUP))).reshape(D_MODEL, KV_HEADS * GATE_ROWS)
    gn = jnp.pad(gn, ((0, 0), (0, LANES - KV_HEADS * GATE_ROWS)))
    cols = [qa, _dup_heads(ka), va, qb, kc, vc, _dup_heads(ks), vs, _dup_heads(kw), vw, gn, ga, gb]
    return jnp.concatenate(cols, axis=1).astype(BF16)


def _inproj(x2, b, s, gain, w_big, cos, sin):
    t = x2.shape[0]
    tm = min(512, s)
    spb = s // tm
    row = lambda w: pl.BlockSpec((tm, w), lambda i: (i, 0))
    fixed = lambda r, c: pl.BlockSpec((r, c), lambda i: (0, 0), pipeline_mode=pl.Buffered(1))
    seq_on_lanes = lambda r: pl.BlockSpec((None, r, tm), lambda i: (i // spb, 0, i % spb))
    outs = [("qa", row(Q_WIDTH), (t, Q_WIDTH), BF16),
            ("ka", row(KV_DUP), (t, KV_DUP), BF16),
            ("vat", seq_on_lanes(KV_WIDTH), (b, KV_WIDTH, s), BF16),
            ("qbraw", row(Q_WIDTH), (t, Q_WIDTH), BF16),
            ("qbrot", row(Q_WIDTH), (t, Q_WIDTH), BF16),
            ("kvc", pl.BlockSpec((2, tm, KV_WIDTH), lambda i: (0, i, 0)), (2, t, KV_WIDTH), F32),
            ("ks", row(KV_DUP), (t, KV_DUP), BF16),
            ("vst", seq_on_lanes(KV_HEADS * SEL_V_ROWS), (b, KV_HEADS * SEL_V_ROWS, s), BF16),
            ("kw", row(KV_DUP), (t, KV_DUP), BF16),
            ("vwt", seq_on_lanes(KV_WIDTH), (b, KV_WIDTH, s), BF16),
            ("gnt", seq_on_lanes(KV_HEADS * GATE_ROWS), (b, KV_HEADS * GATE_ROWS, s), F32),
            ("ga", row(D_MODEL), (t, D_MODEL), BF16),
            ("gb", row(D_MODEL), (t, D_MODEL), BF16)]
    res = pl.pallas_call(
        _inproj_kernel,
        grid=(t // tm,),
        in_specs=[row(D_MODEL), fixed(1, D_MODEL), fixed(D_MODEL, _IN_COLS_BIG), row(LANES), row(LANES)],
        out_specs=[spec for _, spec, _, _ in outs],
        out_shape=[jax.ShapeDtypeStruct(shape, dt) for _, _, shape, dt in outs],
        compiler_params=_params(("parallel",)),
        name="in_proj",
    )(x2, gain.reshape(1, D_MODEL), w_big, cos, sin)
    return {name: o for (name, _, _, _), o in zip(outs, res)}


def _compress_kernel(flat_ref, pe_ref, w1t_ref, w1b_ref, b1_ref, w2_ref, o_ref):
    f = flat_ref[...]
    pe = pe_ref[...]
    top = jnp.dot((f + pe[0:1]).astype(BF16), w1t_ref[...], preferred_element_type=F32)
    bot = jnp.dot((f + pe[1:2]).astype(BF16), w1b_ref[...], preferred_element_type=F32)
    ncp = f.shape[0]
    hid = _gelu_tanh(top + pltpu.roll(bot, ncp - 1, axis=0) + b1_ref[...])
    o_ref[...] = jnp.dot(hid.astype(BF16), w2_ref[...], preferred_element_type=F32).astype(BF16)


def _compress_weights(pos_emb, w1, b1, w2):
    half = CMP_LEN // 2
    eye = jnp.eye(KV_HEADS, dtype=F32)
    w1r = w1.reshape(2, CMP_LEN, HEAD_DIM, CMP_HIDDEN)

    def big(part):
        return jnp.einsum("wtdh,gk->wtgdkh", part, eye).reshape(
            2, half * KV_WIDTH, KV_HEADS * CMP_HIDDEN).astype(BF16)

    def pe_big(part):
        return jnp.broadcast_to(part[:, :, None, :], (2, half, KV_HEADS, HEAD_DIM)).reshape(2, half * KV_WIDTH)

    pe = jnp.stack([pe_big(pos_emb[:, :half]), pe_big(pos_emb[:, half:])], axis=1)
    b1b = jnp.tile(b1, (1, KV_HEADS)).reshape(2, 1, KV_HEADS * CMP_HIDDEN)
    w2b = jnp.einsum("whd,gk,r->wghkrd", w2, eye, jnp.ones((2,), F32)).reshape(
        2, KV_HEADS * CMP_HIDDEN, KV_DUP).astype(BF16)
    return pe, big(w1r[:, :half]), big(w1r[:, half:]), b1b, w2b


def _compress(flat, cw):
    pe, w1t, w1b, b1b, w2b = cw
    _, b, ncp, fw = flat.shape
    hw = KV_HEADS * CMP_HIDDEN
    per_w = lambda *shape: pl.BlockSpec((None,) + shape, lambda w, bb: (w,) + (0,) * len(shape))
    return pl.pallas_call(
        _compress_kernel,
        grid=(2, b),
        in_specs=[pl.BlockSpec((None, None, ncp, fw), lambda w, bb: (w, bb, 0, 0)),
                  per_w(2, fw), per_w(fw, hw), per_w(fw, hw), per_w(1, hw), per_w(hw, KV_DUP)],
        out_specs=pl.BlockSpec((None, None, ncp, KV_DUP), lambda w, bb: (w, bb, 0, 0)),
        out_shape=jax.ShapeDtypeStruct((2, b, ncp, KV_DUP), BF16),
        compiler_params=_params(("parallel", "parallel")),
        name="compress",
    )(flat, pe, w1t, w1b, b1b, w2b)


def _stack_heads(q):
    lane = lax.broadcasted_iota(jnp.int32, (BLOCK, LANES), 1)
    lo = lane < HEAD_DIM
    zero = jnp.zeros((BLOCK, LANES), q.dtype)
    p0, p1 = q[:, :LANES], q[:, LANES:]
    return jnp.concatenate([jnp.where(lo, p0, zero), jnp.where(lo, zero, p0),
                            jnp.where(lo, p1, zero), jnp.where(lo, zero, p1)], axis=0)


def _unstack_out(o_t):
    lane = lax.broadcasted_iota(jnp.int32, (BLOCK, LANES), 1)
    lo = lane < HEAD_DIM
    o2 = jnp.concatenate([o_t, o_t], axis=0)
    t = [o2[:, h * BLOCK:(h + 1) * BLOCK].T for h in range(GROUP)]
    return jnp.concatenate([jnp.where(lo, t[0], t[1]), jnp.where(lo, t[2], t[3])], axis=1)


def _gate_row(g, branch):
    return jnp.concatenate([g[branch * GROUP + h:branch * GROUP + h + 1, :] for h in range(GROUP)], axis=1)


def _nt_dot(a, b):
    return lax.dot_general(a, b, (((1,), (1,)), ((), ())), preferred_element_type=F32)


def _attn_specs(s):
    nb = s // BLOCK
    q_spec = pl.BlockSpec((None, BLOCK, GROUP * HEAD_DIM), lambda b, g, i: (b, i, g))
    k_spec = pl.BlockSpec((None, s, LANES), lambda b, g, i: (b, 0, g))
    vt_spec = pl.BlockSpec((None, HEAD_DIM, s), lambda b, g, i: (b, g, 0))
    gate_spec = pl.BlockSpec((None, None, GATE_ROWS,BLOCK), lambda b, g, i: (b, g, 0, i))
    return nb, q_spec, k_spec, vt_spec, gate_spec


def _band_kernel(*refs, window, nkb, use_sink, gate_branch, has_prev):
    refs = list(refs)
    q_ref, k_ref, vt_ref = refs[:3]
    pos = 3
    g_ref = sink_ref = prev_ref = None
    if gate_branch is not None:
        g_ref = refs[pos]; pos += 1
    if use_sink:
        sink_ref = refs[pos]; pos += 1
    if has_prev:
        prev_ref = refs[pos]; pos += 1
    o_ref = refs[pos]

    length = nkb * BLOCK
    for sub in range(BAND_QBLOCKS):
        i = pl.program_id(2) * BAND_QBLOCKS + sub
        rows = slice(sub * BLOCK, (sub + 1) * BLOCK)
        start = pl.multiple_of(jnp.maximum(i - (nkb - 1), 0) * BLOCK, BLOCK)
        kb = k_ref[pl.ds(start, length), :]
        vb = vt_ref[:, pl.ds(start, length)]
        qs = _stack_heads(q_ref[rows, :])
        kpos = start + lax.broadcasted_iota(jnp.int32, (length, BLOCK), 0)
        qpos = i * BLOCK + lax.broadcasted_iota(jnp.int32, (length, BLOCK), 1)
        rel = qpos - kpos
        bias = jnp.where((rel >= 0) & (rel < window), 0.0, NEG_INF).astype(F32)
        s = _nt_dot(kb, qs) + jnp.concatenate([bias] * GROUP, axis=1)
        m = jnp.max(s, axis=0, keepdims=True)
        if use_sink:
            sink = sink_ref[...]
            m = jnp.maximum(m, sink)
        p = jnp.exp2(s - m)
        l = jnp.sum(p, axis=0, keepdims=True)
        if use_sink:
            l = l + jnp.exp2(sink - m)
        o_t = jnp.dot(vb, p.astype(BF16), preferred_element_type=F32)
        scale = 1.0 / jnp.maximum(l, TINY)
        if gate_branch is not None:
            scale = scale * _gate_row(g_ref[:, rows], gate_branch)
        out = _unstack_out(o_t * scale)
        if has_prev:
            out = out + prev_ref[rows, :]
        o_ref[rows, :] = out.astype(o_ref.dtype)


def _band_attention(q, k, vt, *, window, gates=None, gate_branch=None, sinks=None, prev=None, out_dtype=F32):
    b, s, _ = q.shape
    nb, _, k_spec, vt_spec, _ = _attn_specs(s)
    rows = BAND_QBLOCKS * BLOCK
    q_spec = pl.BlockSpec((None, rows, GROUP * HEAD_DIM), lambda b_, g, i: (b_, i, g))
    gate_spec = pl.BlockSpec((None, None, GATE_ROWS,rows), lambda b_, g, i: (b_, g, 0, i))
    nkb = -(-window // BLOCK) + 1
    ins, specs = [q, k, vt], [q_spec, k_spec, vt_spec]
    if gate_branch is not None:
        ins.append(gates); specs.append(gate_spec)
    if sinks is not None:
        ins.append(sinks)
        specs.append(pl.BlockSpec((None, 1, GROUP * BLOCK), lambda b_, g, i: (g, 0, 0)))
    if prev is not None:
        ins.append(prev); specs.append(q_spec)
    kern = functools.partial(_band_kernel, window=window, nkb=nkb, use_sink=sinks is not None,
                             gate_branch=gate_branch, has_prev=prev is not None)
    return pl.pallas_call(
        kern,
        grid=(b, KV_HEADS, nb // BAND_QBLOCKS),
        in_specs=specs,
        out_specs=q_spec,
        out_shape=jax.ShapeDtypeStruct((b, s, Q_WIDTH), out_dtype),
        compiler_params=_params(("parallel", "parallel", "parallel")),
        name=f"band_attn_w{window}",
    )(*ins)


def _cmp_kernel(q_ref, kc_ref, vct_ref, g_ref, mt_ref, o_ref, sel_ref, *, n_sel):
    step = pl.program_id(2)
    steps = pl.num_programs(2)
    ncp_all, ns_all = kc_ref.shape[0], mt_ref.shape[0]

    def prefix(ncp, ns):
        for sub in range(CMP_QBLOCKS):
            _cmp_block(step * CMP_QBLOCKS + sub, slice(sub * BLOCK, (sub + 1) * BLOCK),
                       q_ref, kc_ref, vct_ref, g_ref, mt_ref, o_ref, sel_ref, n_sel=n_sel, ncp=ncp, ns=ns)

    for k in range(CMP_PREFIXES):
        ncp = ncp_all * (k + 1) // CMP_PREFIXES
        ns = min(ns_all, -(-(ns_all * (k + 1) // CMP_PREFIXES) // LANES) * LANES)
        pl.when(step * CMP_PREFIXES // steps == k)(functools.partial(prefix, ncp, ns))


def _cmp_block(i, rows, q_ref, kc_ref, vct_ref, g_ref, mt_ref, o_ref, sel_ref, *, n_sel, ncp, ns):
    qs = _stack_heads(q_ref[rows, :])
    c = lax.broadcasted_iota(jnp.int32, (ncp, BLOCK), 0)
    t = i * BLOCK + lax.broadcasted_iota(jnp.int32, (1, BLOCK), 1)
    last = (t - (CMP_LEN - 1)) >> (CMP_STRIDE.bit_length() - 1)
    bias = jnp.where(c <= last, 0.0, NEG_INF).astype(F32)
    s = _nt_dot(kc_ref[0:ncp, :], qs) + jnp.concatenate([bias] * GROUP, axis=1)
    m = jnp.max(s, axis=0, keepdims=True)
    m = jnp.where(m == NEG_INF, 0.0, m)
    p = jnp.exp2(s - m)
    l = jnp.sum(p, axis=0, keepdims=True)
    p = p * (1.0 / jnp.maximum(l, TINY))
    o_t = jnp.dot(vct_ref[:, 0:ncp], p.astype(BF16), preferred_element_type=F32)
    o_ref[rows, :] = _unstack_out(o_t * _gate_row(g_ref[:, rows], 0))

    imp = p[:, 0:BLOCK]
    for h in range(1, GROUP):
        imp = imp + p[:, h * BLOCK:(h + 1) * BLOCK]
    hi = imp.astype(BF16)
    r1 = imp - hi.astype(F32)
    mid = r1.astype(BF16)
    lo = (r1 - mid.astype(F32)).astype(BF16)
    mt = mt_ref[0:ns, 0:ncp]
    imp_s = (jnp.dot(mt, hi, preferred_element_type=F32) + jnp.dot(mt, mid, preferred_element_type=F32)
             + jnp.dot(mt, lo, preferred_element_type=F32))

    j =lax.broadcasted_iota(jnp.int32, imp_s.shape, 0)
    tq = i * BLOCK + lax.broadcasted_iota(jnp.int32, imp_s.shape, 1)
    cur = tq >> (SLC_LEN.bit_length() - 1)
    valid = j <= cur
    forced = (j == 0) | (valid & (j > cur - N_LOCAL_SLC))
    work = jnp.where(valid & jnp.logical_not(forced), imp_s, NEG_INF)
    sel_bias = jnp.where(forced, 0.0, NEG_BIG).astype(F32)
    for _ in range(n_sel - 1 - N_LOCAL_SLC):
        mx = jnp.max(work, axis=0, keepdims=True)
        first = jnp.min(jnp.where(work == mx, j, ns), axis=0, keepdims=True)
        pick = j == first
        sel_bias = jnp.where(pick, 0.0, sel_bias)
        work = jnp.where(pick, NEG_INF, work)
    sel_bias = jnp.where(valid, sel_bias, NEG_BIG)
    sel_q = sel_bias.T.astype(BF16)
    pad = sel_ref.shape[1] - ns
    sel_ref[rows, :] = jnp.concatenate([sel_q, jnp.full((BLOCK, pad), NEG_BIG, BF16)], axis=1)


def _overlap_map(ns, ncp):
    ratio_c = CMP_LEN // CMP_STRIDE
    ratio_s = SLC_LEN // CMP_STRIDE
    m = np.zeros((ns, ncp), np.float32)
    for jj in range(ns):
        for a in range(ratio_s):
            for n in range(ratio_c):
                c = ratio_s * jj + a - n
                if 0 <= c < ncp - ratio_c + 1:
                    m[jj, c] += 1.0
    return jnp.asarray(m, BF16)


def _cmp_attention(q_raw, kc, vct, gates):
    b, s, _ = q_raw.shape
    nb = s // BLOCK
    rows = CMP_QBLOCKS * BLOCK
    assert nb % (CMP_QBLOCKS * CMP_PREFIXES) == 0
    q_spec = pl.BlockSpec((None, rows, GROUP * HEAD_DIM), lambda b_, g, i: (b_, i, g))
    gate_spec = pl.BlockSpec((None, None, GATE_ROWS,rows), lambda b_, g, i: (b_, g, 0, i))
    ncp = s // CMP_STRIDE
    ns = s // SLC_LEN
    assert ns >= N_SLC
    nsp = -(-ns // LANES) * LANES + LANES
    mt = _overlap_map(ns, ncp)
    kern = functools.partial(_cmp_kernel, n_sel=N_SLC)
    return pl.pallas_call(
        kern,
        grid=(b, KV_HEADS, nb // CMP_QBLOCKS),
        in_specs=[q_spec,
                  pl.BlockSpec((None, ncp, LANES), lambda b_, g, i: (b_, 0, g)),
                  pl.BlockSpec((None, HEAD_DIM, ncp), lambda b_, g, i: (b_, 2 * g, 0)),
                  gate_spec,
                  pl.BlockSpec((ns, ncp), lambda b_, g, i: (0, 0))],
        out_specs=[q_spec, pl.BlockSpec((None, None, rows, nsp), lambda b_, g, i: (b_, g, i, 0))],
        out_shape=[jax.ShapeDtypeStruct((b, s, Q_WIDTH), F32),
                   jax.ShapeDtypeStruct((b, KV_HEADS, s, nsp), BF16)],
        compiler_params=_params(("parallel", "parallel", "parallel")),
        name="cmp_attn_topk",
    )(q_raw, kc, vct, gates, mt)


CMP_PREFIXES = 8
CMP_QBLOCKS = 4
SEL_BLOCKS_PER_TILE = SEL_TILE // SLC_LEN
SEL_TILES_PER_LANES = LANES // SEL_BLOCKS_PER_TILE
SEL_DIAG_VARIANTS = 8
SEL_V_ROWS = HEAD_DIM + 16


def _sel_kernel(q_ref, k_ref, vt_ref, sel_ref, oh_ref, g_ref, prev_ref, o_ref,
                s0_sc, sa_sc, sb_sc, mxa_sc, mxb_sc, ra_sc, rb_sc, m_sc, acc_sc):
    i = pl.program_id(2)
    tk = SEL_TILE
    qs = _stack_heads(q_ref[...])
    diag = i // (tk // BLOCK)
    masked_lanes = sel_ref.shape[1] - LANES
    bufs = ((sa_sc, mxa_sc, ra_sc), (sb_sc, mxb_sc, rb_sc))

    def scores(tile, lane0):
        k0 = pl.multiple_of(tile * tk, tk)
        feat = sel_ref[:, pl.ds(pl.multiple_of(lane0, LANES), LANES)]
        qa = jnp.concatenate([qs, jnp.concatenate([feat] * GROUP, axis=0)], axis=1)
        ka = jnp.concatenate([k_ref[pl.ds(k0, tk), :], oh_ref[tile % SEL_TILES_PER_LANES]], axis=1)
        return _nt_dot(ka, qa)

    def values(tile):
        return vt_ref[:, pl.ds(pl.multiple_of(tile * tk, tk), tk)]

    def tile_or_pad(tile):
        return jnp.minimum(tile, jnp.maximum(diag - 1, 0))

    def produce(tile, buf):
        tc = tile_or_pad(tile)
        lane0 = jnp.where(tile < diag, (tc // SEL_TILES_PER_LANES) * LANES, masked_lanes)
        sn = scores(tc, lane0)
        ref = m_sc[...]
        buf[0][...] = (sn - ref).astype(BF16)
        buf[1][...] = jnp.max(sn, axis=0, keepdims=True)
        buf[2][...] = ref

    def consume(tile, buf):
        m_old = m_sc[...]
        m_new = jnp.maximum(m_old, buf[1][...])
        alpha = jnp.exp2(m_old - m_new)
        shift = (buf[2][...] - m_new).astype(BF16)
        p = jnp.exp2(buf[0][...] + shift)
        acc_sc[...] = alpha * acc_sc[...] + jnp.dot(values(tile_or_pad(tile)), p, preferred_element_type=F32)
        m_sc[...] = m_new

    def diagonal(rows):
        k0 = pl.multiple_of(diag * tk, tk)
        lane0 = pl.multiple_of((diag // SEL_TILES_PER_LANES) * LANES, LANES)
        qa = jnp.concatenate([qs, jnp.concatenate([sel_ref[:, pl.ds(lane0, LANES)]] * GROUP, axis=0)], axis=1)
        ka = jnp.concatenate([k_ref[pl.ds(k0, rows), :], oh_ref[diag % SEL_TILES_PER_LANES, 0:rows, :]], axis=1)
        kpos = k0 + lax.broadcasted_iota(jnp.int32, (rows, BLOCK), 0)
        qpos = i * BLOCK + lax.broadcasted_iota(jnp.int32, (rows, BLOCK), 1)
        causal = jnp.where(kpos <= qpos, 0.0, NEG_BIG).astype(F32)
        s0 = _nt_dot(ka, qa) + jnp.concatenate([causal] * GROUP, axis=1)
        m0 = jnp.max(s0, axis=0, keepdims=True)
        m_sc[...] = m0
        s0_sc[0:rows, :] = s0
        produce(0, bufs[0])
        p0 = jnp.exp2(s0_sc[0:rows, :] - m0).astype(BF16)
        acc_sc[...] = jnp.dot(vt_ref[:, pl.ds(k0, rows)], p0, preferred_element_type=F32)

    blocks_per_tile = tk // BLOCK
    for v in range(SEL_DIAG_VARIANTS):
        rows = tk * (v + 1) // SEL_DIAG_VARIANTS
        pl.when((i % blocks_per_tile) * SEL_DIAG_VARIANTS // blocks_per_tile == v)(
            functools.partial(diagonal, rows))

    def body(pair, carry):
        tile = 2 * pair
        produce(tile + 1, bufs[1])
        consume(tile, bufs[0])
        produce(tile + 2, bufs[0])
        consume(tile + 1, bufs[1])
        return carry

    lax.fori_loop(0, (diag + 1) // 2, body, 0)
    acc = acc_sc[...]
    scale = _gate_row(g_ref[...], 1) / jnp.maximum(acc[HEAD_DIM:HEAD_DIM + 1, :], TINY)
    o_ref[...] = _unstack_out(acc[0:HEAD_DIM, :] * scale) + prev_ref[...]


def _block_onehots():
    jm = np.arange(SEL_TILES_PER_LANES)[:, None, None]
    r = np.arange(SEL_TILE)[None, :, None]
    c = np.arange(LANES)[None, None, :]
    return jnp.asarray(c == jm * SEL_BLOCKS_PER_TILE + r // SLC_LEN, BF16)


def _sel_attention(q_rot, ks, vst, sel, gates, prev):
    b, s, _ = q_rot.shape
    nb, q_spec, k_spec, _, gate_spec = _attn_specs(s)
    vt_spec = pl.BlockSpec((None, SEL_V_ROWS, s), lambda b_, g, i: (b_, g, 0))
    nsp = sel.shape[-1]
    cols = GROUP * BLOCK
    return pl.pallas_call(
        _sel_kernel,
        grid=(b, KV_HEADS, nb),
        in_specs=[q_spec, k_spec, vt_spec,
                  pl.BlockSpec((None, None, BLOCK, nsp), lambda b_, g, i: (b_, g, i, 0)),
                  pl.BlockSpec((SEL_TILES_PER_LANES, SEL_TILE, LANES), lambda b_, g, i: (0, 0, 0)),
                  gate_spec, q_spec],
        out_specs=q_spec,
        out_shape=jax.ShapeDtypeStruct((b, s, Q_WIDTH), F32),
        scratch_shapes=[pltpu.VMEM((SEL_TILE, cols), F32),
                        pltpu.VMEM((SEL_TILE, cols), BF16), pltpu.VMEM((SEL_TILE, cols), BF16),
                        pltpu.VMEM((1, cols), F32), pltpu.VMEM((1, cols), F32),
                        pltpu.VMEM((1, cols), F32), pltpu.VMEM((1, cols), F32),
                        pltpu.VMEM((1, cols), F32),
                        pltpu.VMEM((SEL_V_ROWS, cols), F32)],
        compiler_params=_params(("parallel", "parallel", "parallel")),
        name="sel_attn",
    )(q_rot, ks, vst, sel, _block_onehots(), gates, prev)


def _merge_kernel(oa_ref, ob_ref, ga_ref, gb_ref, x_ref, wa_ref, wb_ref, wo_ref, gain_ref, o_ref):
    ya = jnp.dot(oa_ref[...], wa_ref[...], preferred_element_type=F32)
    yb = jnp.dot(ob_ref[...].astype(BF16), wb_ref[...], preferred_element_type=F32)
    mixed = ga_ref[...].astype(F32) * ya + gb_ref[...].astype(F32) * yb
    z = jnp.dot(mixed.astype(BF16), wo_ref[...], preferred_element_type=F32)
    o_ref[...] = x_ref[...] + _rms(z, gain_ref[...])


def _merge(o_a, o_b, ga, gb, x2, w_a, w_b, w_o, gain):
    t = x2.shape[0]
    tm = 512
    row = lambda w: pl.BlockSpec((tm, w), lambda i: (i, 0))
    fixed = lambda r, c: pl.BlockSpec((r, c), lambda i: (0, 0))
    return pl.pallas_call(
        _merge_kernel,
        grid=(t // tm,),
        in_specs=[row(Q_WIDTH), row(Q_WIDTH), row(D_MODEL), row(D_MODEL), row(D_MODEL),
                  fixed(Q_WIDTH, D_MODEL), fixed(Q_WIDTH, D_MODEL), fixed(D_MODEL, D_MODEL), fixed(1, D_MODEL)],
        out_specs=row(D_MODEL),
        out_shape=jax.ShapeDtypeStruct((t, D_MODEL), F32),
        compiler_params=_params(("parallel",)),
        name="merge_out_proj",
    )(o_a, o_b, ga, gb, x2, w_a.astype(BF16), w_b.astype(BF16), w_o.astype(BF16), gain.reshape(1, D_MODEL))


def _ffn_kernel(xp_ref, x_ref, gpre_ref, wu_ref, cw_ref, cb_ref, wd_ref, gpost_ref, o_ref,
                h_sc, ua_sc, ub_sc, g_sc, *, tm, tiles_per_seq, nck):
    i = pl.program_id(0)
    gpre = gpre_ref[...]
    keep = (i % tiles_per_seq != 0).astype(F32)
    h_sc[0:FFN_HALO, :] = (_rms(xp_ref[...], gpre) * keep).astype(BF16)
    h_sc[FFN_HALO:, :] = _rms(x_ref[...], gpre).astype(BF16)

    def produce(n, u_sc):
        h = h_sc[...]
        u_sc[0] = jnp.dot(h, wu_ref[n], preferred_element_type=F32)
        u_sc[1] = jnp.dot(h, wu_ref[nck + n], preferred_element_type=F32)

    def consume(n, u_sc):
        def conv(part, col):
            cw = cw_ref[col]
            return (cb_ref[col] + cw[0:1] * u_sc[part, pl.ds(FFN_HALO - 2, tm), :]
                    + cw[1:2] * u_sc[part, pl.ds(FFN_HALO - 1, tm), :]
                    + cw[2:3] * u_sc[part, pl.ds(FFN_HALO, tm), :])

        g_sc[n] = (_gelu_tanh(conv(0, n)) * conv(1, nck + n)).astype(BF16)

    bufs = (ua_sc, ub_sc)
    produce(0, bufs[0])
    for n in range(nck):
        if n + 1 < nck:
            produce(n + 1, bufs[(n + 1) % 2])
        consume(n, bufs[n % 2])
    g = jnp.concatenate([g_sc[n] for n in range(nck)], axis=1)
    z = jnp.dot(g, wd_ref[...], preferred_element_type=F32)
    o_ref[...] = x_ref[...] + _rms(z, gpost_ref[...])


def _ffn(x2, seq, gpre, w_up, conv_w, conv_b, w_down, gpost):
    t = x2.shape[0]
    tm = min(512, seq)
    ck = FFN_CHUNK
    nck = D_FF // ck
    assert nck % 2 == 1
    kern = functools.partial(_ffn_kernel, tm=tm, tiles_per_seq=seq // tm, nck=nck)
    halo_blocks = tm // FFN_HALO
    chunks = lambda a: jnp.transpose(a.reshape(a.shape[0], 2 * nck, ck), (1, 0, 2))
    whole = lambda *shape: pl.BlockSpec(shape, lambda i: (0,) * len(shape), pipeline_mode=pl.Buffered(1))
    return pl.pallas_call(
        kern,
        grid=(t // tm,),
        in_specs=[pl.BlockSpec((FFN_HALO, D_MODEL), lambda i: (jnp.maximum(i * halo_blocks - 1, 0), 0)),
                  pl.BlockSpec((tm, D_MODEL), lambda i: (i, 0)),
                  whole(1, D_MODEL),
                  whole(2 * nck, D_MODEL, ck),
                  whole(2 * nck, 3, ck),
                  whole(2 * nck, 1, ck),
                  whole(D_FF, D_MODEL),
                  whole(1, D_MODEL)],
        out_specs=pl.BlockSpec((tm, D_MODEL), lambda i: (i, 0)),
        out_shape=jax.ShapeDtypeStruct((t, D_MODEL), F32),
        scratch_shapes=[pltpu.VMEM((tm + FFN_HALO, D_MODEL), BF16),
                        pltpu.VMEM((2, tm + FFN_HALO, ck), F32),
                        pltpu.VMEM((2, tm + FFN_HALO, ck), F32),
                        pltpu.VMEM((nck, tm, ck), BF16)],
        compiler_params=_params(("parallel",)),
        name="conv_ffn",
    )(x2, x2, gpre.reshape(1, D_MODEL), chunks(w_up.astype(BF16)), chunks(conv_w),
      chunks(conv_b.reshape(1, 2 * D_FF)), w_down.astype(BF16), gpost.reshape(1, D_MODEL))


def _layer(x2, b, s, cos, sin, p):
    t = b * s
    u = _inproj(x2, b, s, p["attn_pre_gain"], _inproj_weight(p["w_in"]), cos, sin)
    r3 = lambda a: a.reshape(b, s, a.shape[-1])

    gates = u["gnt"].reshape(b, KV_HEADS, GATE_ROWS, s)
    sinks = jnp.repeat(p["attn_sinks"].astype(F32) * LOG2E, BLOCK).reshape(KV_HEADS, 1, GROUP * BLOCK)

    o_a = _band_attention(r3(u["qa"]), r3(u["ka"]), u["vat"], window=A_WINDOW, sinks=sinks, out_dtype=BF16)

    ncp = s // CMP_STRIDE
    flat = u["kvc"].reshape(2, b, ncp, CMP_STRIDE * KV_WIDTH)
    cmp = _compress(flat, _compress_weights(p["cmp_pos_emb"], p["cmp_w1"], p["cmp_b1"], p["cmp_w2"]))
    o_c, sel = _cmp_attention(r3(u["qbraw"]), cmp[0], jnp.transpose(cmp[1], (0, 2, 1)), gates)
    o_cs = _sel_attention(r3(u["qbrot"]), r3(u["ks"]), u["vst"], sel, gates, o_c)
    o_b = _band_attention(r3(u["qbrot"]), r3(u["kw"]), u["vwt"], window=B_WINDOW,
                          gates=gates, gate_branch=2, prev=o_cs)

    x2 = _merge(o_a.reshape(t, Q_WIDTH), o_b.reshape(t, Q_WIDTH), u["ga"], u["gb"], x2,
                p["w_branch_a"], p["w_branch_b"], p["w_out"], p["attn_post_gain"])
    return _ffn(x2, s, p["ffn_pre_gain"], p["ffn_w_up"], p["ffn_conv_w"], p["ffn_conv_b"], p["ffn_w_down"],
                p["ffn_post_gain"])


def kernel(x, positions, attn_pre_gain, attn_post_gain, ffn_pre_gain, ffn_post_gain, w_in, attn_sinks,
           cmp_pos_emb, cmp_w1, cmp_b1, cmp_w2, w_branch_a, w_branch_b, w_out, ffn_w_up, ffn_conv_w,
           ffn_conv_b, ffn_w_down):
    b, s, d = x.shape
    params = dict(attn_pre_gain=attn_pre_gain, attn_post_gain=attn_post_gain, ffn_pre_gain=ffn_pre_gain,
                  ffn_post_gain=ffn_post_gain, w_in=w_in, attn_sinks=attn_sinks, cmp_pos_emb=cmp_pos_emb,
                  cmp_w1=cmp_w1, cmp_b1=cmp_b1, cmp_w2=cmp_w2, w_branch_a=w_branch_a, w_branch_b=w_branch_b,
                  w_out=w_out, ffn_w_up=ffn_w_up, ffn_conv_w=ffn_conv_w, ffn_conv_b=ffn_conv_b,
                  ffn_w_down=ffn_w_down)
    cos, sin = _rope_tables(positions)
    x2 = x.reshape(b * s, d)
    for layer in range(attn_pre_gain.shape[0]):
        x2 = _layer(x2, b, s, cos, sin, {k: v[layer] for k, v in params.items()})
    return x2.reshape(b, s, d)
```
